```python
import jax, jax.numpy as jnp
from jax import lax
import numpy as np

D_MODEL = 2048
BATCH = 1
SEQ = 16384
DEPTH = 1

CHUNK = 64

D_LRU = D_MODEL
LRU_BLOCKS = 16
LRU_BLOCK_W = D_LRU // LRU_BLOCKS
CONV_W = 4
LRU_C = 8.0

SB_HEADS = 16
SB_HEAD_DIM = 128
D_SB = SB_HEADS * SB_HEAD_DIM
Q_BLOCK = 128

D_CAT = D_LRU + D_SB
N_IN = 2 * D_LRU + 4 * D_SB + D_CAT
EPS = 1e-6

kernel_name = "hybrid_rglru_stickbreaking_gated_merge"


def rmsnorm(x, g):
    xf = x.astype(jnp.float32)
    y = xf * lax.rsqrt(jnp.mean(xf * xf, axis=-1, keepdims=True) + EPS)
    return (y * g.astype(jnp.float32)).astype(x.dtype)


def causal_depthwise_conv(x, w, b):
    K = w.shape[0]
    S = x.shape[1]
    xp = jnp.pad(x, ((0, 0), (K - 1, 0), (0, 0)))
    y = b
    for tap in range(K):
        y = y + xp[:, tap:tap + S] * w[tap]
    return y


def rg_lru(xc, w_rg, b_rg, w_ig, b_ig, lam):
    B, S, C = xc.shape
    xf = xc.astype(jnp.float32)
    xb = xf.reshape(B, S, LRU_BLOCKS, LRU_BLOCK_W)
    r = jax.nn.sigmoid(jnp.einsum('bshi,hij->bshj', xb, w_rg.astype(jnp.float32)).reshape(B, S, C) + b_rg.astype(jnp.float32))
    i = jax.nn.sigmoid(jnp.einsum('bshi,hij->bshj', xb, w_ig.astype(jnp.float32)).reshape(B, S, C) + b_ig.astype(jnp.float32))
    log_a = LRU_C * r * jax.nn.log_sigmoid(lam.astype(jnp.float32))
    a = jnp.exp(log_a)
    u = jnp.sqrt(-jnp.expm1(2.0 * log_a)) * (i * xf)

    def combine(c1, c2):
        a1, b1 = c1
        a2, b2 = c2
        return a1 * a2, a2 * b1 + b2

    _, h = lax.associative_scan(combine, (a, u), axis=1)
    return h.astype(xc.dtype)


def stick_breaking_attention(q, k, v):
    S = q.shape[2]
    scale = SB_HEAD_DIM ** -0.5
    outs = []
    for blk in range(S // Q_BLOCK):
        q0 = blk * Q_BLOCK
        end = q0 + Q_BLOCK
        qb = q[:, :, q0:end].astype(jnp.float32)
        kb = k[:, :, :end].astype(jnp.float32)
        vb = v[:, :, :end].astype(jnp.float32)
        z = jnp.einsum('bhqd,bhkd->bhqk', qb, kb) * scale
        t_q = q0 + jnp.arange(Q_BLOCK)[:, None]
        t_k = jnp.arange(end)[None, :]
        mask = t_k < t_q
        log_rem = jnp.where(mask, jax.nn.log_sigmoid(-z), 0.0)
        between = lax.cumsum(log_rem, axis=3, reverse=True) - log_rem
        w = jnp.where(mask, jnp.exp(jax.nn.log_sigmoid(z) + between), 0.0)
        outs.append(jnp.einsum('bhqk,bhkd->bhqd', w, vb))
    return jnp.concatenate(outs, axis=2).astype(q.dtype)


def setup_inputs(seed: int = 0) -> dict:
    key = jax.random.key(seed)
    ks = jax.random.split(key, 13)
    f32 = jnp.float32
    x = jax.random.normal(ks[0], (BATCH, SEQ, D_MODEL), f32)
    norm_g = 1.0 + 0.01 * jax.random.normal(ks[1], (DEPTH, D_MODEL), f32)
    w_in = jax.random.normal(ks[2], (DEPTH, D_MODEL, N_IN), f32) * D_MODEL ** -0.5
    conv_w = jax.random.normal(ks[3], (DEPTH, CONV_W, D_LRU), f32) * CONV_W ** -0.5
    conv_b = 0.01 * jax.random.normal(ks[4], (DEPTH, D_LRU), f32)
    w_rg = jax.random.normal(ks[5], (DEPTH, LRU_BLOCKS, LRU_BLOCK_W, LRU_BLOCK_W), f32) * LRU_BLOCK_W ** -0.5
    b_rg = 0.01 * jax.random.normal(ks[6], (DEPTH, D_LRU), f32)
    w_ig = jax.random.normal(ks[7], (DEPTH, LRU_BLOCKS, LRU_BLOCK_W, LRU_BLOCK_W), f32) * LRU_BLOCK_W ** -0.5
    b_ig = 0.01 * jax.random.normal(ks[8], (DEPTH, D_LRU), f32)
    u = jax.random.uniform(ks[9], (DEPTH, D_LRU), f32, minval=0.9, maxval=0.999)
    p = u ** (1.0 / LRU_C)
    lru_lambda = jnp.log(p) - jnp.log1p(-p)
    b_merge = 0.01 * jax.random.normal(ks[10], (DEPTH, D_CAT), f32)
    w_out = jax.random.normal(ks[11], (DEPTH, D_CAT, D_MODEL), f32) * D_CAT ** -0.5
    final_g = 1.0 + 0.01 * jax.random.normal(ks[12], (D_MODEL,), f32)
    return {"x": x, "norm_g": norm_g, "w_in": w_in, "conv_w": conv_w, "conv_b": conv_b,
            "w_rg": w_rg, "b_rg": b_rg, "w_ig": w_ig, "b_ig": b_ig, "lru_lambda": lru_lambda,
            "b_merge": b_merge, "w_out": w_out, "final_g": final_g}


def reference(x, norm_g, w_in, conv_w, conv_b, w_rg, b_rg, w_ig, b_ig, lru_lambda, b_merge, w_out, final_g):
    B, S, _ = x.shape
    for l in range(DEPTH):
        h = rmsnorm(x, norm_g[l])
        proj = jnp.einsum('bsd,dn->bsn', h, w_in[l])
        offs = np.cumsum([0, D_LRU, D_LRU, D_SB, D_SB, D_SB, D_SB, D_CAT])
        lru_x, lru_gate, q, k, v, sb_gate, merge_logits = [proj[..., offs[j]:offs[j + 1]] for j in range(7)]

        xc = causal_depthwise_conv(lru_x, conv_w[l], conv_b[l])
        y_lru = rg_lru(xc, w_rg[l], b_rg[l], w_ig[l], b_ig[l], lru_lambda[l]) * jax.nn.silu(lru_gate)

        to_heads = lambda t: t.reshape(B, S, SB_HEADS, SB_HEAD_DIM).transpose(0, 2, 1, 3)
        o = stick_breaking_attention(to_heads(q), to_heads(k), to_heads(v))
        y_sb = o.transpose(0, 2, 1, 3).reshape(B, S, D_SB) * jax.nn.silu(sb_gate)

        g = jax.nn.sigmoid(merge_logits + b_merge[l])
        merged = jnp.concatenate([y_lru, y_sb], axis=-1) * g
        x = x + jnp.einsum('bsc,cd->bsd', merged, w_out[l])
    return rmsnorm(x, final_g)
```

```python
import functools

import jax
import jax.numpy as jnp
from jax import lax
from jax.experimental import pallas as pl
from jax.experimental.pallas import tpu as pltpu

D_MODEL = 2048
D_LRU = 2048
LRU_BLOCKS = 16
LRU_BLOCK_W = 128
CONV_W = 4
LRU_C = 8.0
SB_HEADS = 16
SB_HEAD_DIM = 128
D_SB = SB_HEADS * SB_HEAD_DIM
D_CAT = D_LRU + D_SB
N_IN = 2 * D_LRU + 4 * D_SB + D_CAT
EPS = 1e-6

OFF_LRU_X = 0
OFF_LRU_GATE = OFF_LRU_X + D_LRU
OFF_Q = OFF_LRU_GATE + D_LRU
OFF_K = OFF_Q + D_SB
OFF_V = OFF_K + D_SB
OFF_SB_GATE = OFF_V + D_SB
OFF_MERGE_A = OFF_SB_GATE + D_SB
OFF_MERGE_B = OFF_MERGE_A + D_LRU

LANE = 128
SUBLANE = 8
VMEM_LIMIT = 56 * 1024 * 1024

EXIT_LOG = -104.0

F32 = jnp.float32
BF16 = jnp.bfloat16


def _softplus(z):
    return jnp.maximum(z, 0.0) + jnp.log(1.0 + jnp.exp(-jnp.abs(z)))


def _sigmoid(z):
    return jax.nn.sigmoid(z)


def _silu(z):
    return z * jax.nn.sigmoid(z)


def _inproj_kernel(x_ref, g_ref, w_ref, o_ref, h_ref):
    @pl.when(pl.program_id(1) == 0)
    def _():
        x = x_ref[...]
        ms = jnp.mean(x * x, axis=-1, keepdims=True)
        h_ref[...] = (x * lax.rsqrt(ms + EPS) * g_ref[...]).astype(BF16)

    o_ref[...] = jnp.dot(h_ref[...], w_ref[...], preferred_element_type=F32).astype(BF16)


def _inproj(x, norm_g, w_bf16, tm=1024, tn=1024):
    S = x.shape[0]
    return pl.pallas_call(
        _inproj_kernel,
        grid=(S // tm, N_IN // tn),
        in_specs=[
            pl.BlockSpec((tm, D_MODEL), lambda i, j: (i, 0)),
            pl.BlockSpec((1, D_MODEL), lambda i, j: (0, 0)),
            pl.BlockSpec((D_MODEL, tn), lambda i, j: (0, j)),
        ],
        out_specs=pl.BlockSpec((tm, tn), lambda i, j: (i, j)),
        out_shape=jax.ShapeDtypeStruct((S, N_IN), BF16),
        scratch_shapes=[pltpu.VMEM((tm, D_MODEL), BF16)],
        compiler_params=pltpu.CompilerParams(
            dimension_semantics=("arbitrary", "arbitrary"), vmem_limit_bytes=VMEM_LIMIT),
        name="inproj",
    )(x, norm_g, w_bf16)


def _lru_kernel(px_ref, pg_ref, pm_ref, cw_ref, cb_ref, wg_ref, brg_ref, big_ref, lam_ref, bm_ref,
                y_ref, xs_ref, a_ref, u_ref, hc_ref, *, tl):
    i = pl.program_id(0)

    @pl.when(i == 0)
    def _():
        xs_ref[0:SUBLANE, :] = jnp.zeros((SUBLANE, D_LRU), F32)
        hc_ref[...] = jnp.zeros((SUBLANE, D_LRU), F32)

    xs_ref[SUBLANE:SUBLANE + tl, :] = px_ref[...].astype(F32)
    xc = cb_ref[...]
    for tap in range(CONV_W):
        start = SUBLANE - (CONV_W - 1) + tap
        xc = xc + xs_ref[start:start + tl, :] * cw_ref[tap:tap + 1, :]
    xs_ref[0:SUBLANE, :] = xs_ref[tl:tl + SUBLANE, :]

    lam = lam_ref[...]
    log_sig_lam = -(jnp.maximum(-lam, 0.0) + jnp.log1p(jnp.exp(-jnp.abs(lam))))
    row = lax.broadcasted_iota(jnp.int32, (tl, LRU_BLOCK_W), 0) & (SUBLANE - 1)

    for c in range(LRU_BLOCKS):
        cs = slice(c * LRU_BLOCK_W, (c + 1) * LRU_BLOCK_W)
        xcb = xc[:, cs]
        gates = jnp.dot(xcb.astype(BF16), wg_ref[c], preferred_element_type=F32)
        r = _sigmoid(gates[:, :LRU_BLOCK_W] + brg_ref[:, cs])
        ig = _sigmoid(gates[:, LRU_BLOCK_W:] + big_ref[:, cs])
        log_a = (LRU_C * log_sig_lam[:, cs]) * r
        a = jnp.exp(log_a)
        t = jnp.tanh(log_a)
        u = jnp.sqrt(-2.0 * t / (1.0 - t)) * (ig * xcb)
        for d in (1, 2, 4):
            keep = row >= d
            u = u + a * jnp.where(keep, pltpu.roll(u, d, axis=0), 0.0)
            a = a * jnp.where(keep, pltpu.roll(a, d, axis=0), 1.0)
        a_ref[:, cs] = a
        u_ref[:, cs] = u

    def carry(k, h_prev):
        r0 = pl.multiple_of(k * SUBLANE, SUBLANE)
        h = u_ref[pl.ds(r0, SUBLANE), :] + a_ref[pl.ds(r0, SUBLANE), :] * h_prev
        u_ref[pl.ds(r0, SUBLANE), :] = h
        return jnp.broadcast_to(h[SUBLANE - 1:SUBLANE, :], (SUBLANE, D_LRU))

    hc_ref[...] = lax.fori_loop(0, tl // SUBLANE, carry, hc_ref[...])

    gate = _silu(pg_ref[...].astype(F32)) * _sigmoid(pm_ref[...].astype(F32) + bm_ref[...])
    y_ref[...] = (u_ref[...] * gate).astype(BF16)


def _lru(proj, conv_w, conv_b, w_gates, b_rg, b_ig, lam, b_merge, tl=256):
    S = proj.shape[0]
    row_spec = lambda off: pl.BlockSpec((tl, D_LRU), lambda i, o=off // D_LRU: (i, o))
    vec_spec = pl.BlockSpec((1, D_LRU), lambda i: (0, 0))
    return pl.pallas_call(
        functools.partial(_lru_kernel, tl=tl),
        grid=(S // tl,),
        in_specs=[
            row_spec(OFF_LRU_X), row_spec(OFF_LRU_GATE), row_spec(OFF_MERGE_A),
            pl.BlockSpec((CONV_W, D_LRU), lambda i: (0, 0)),
            vec_spec,
            pl.BlockSpec((LRU_BLOCKS, LRU_BLOCK_W, 2 * LRU_BLOCK_W), lambda i: (0, 0, 0)),
            vec_spec, vec_spec, vec_spec,
            pl.BlockSpec((1, D_LRU), lambda i: (0, 0)),
        ],
        out_specs=pl.BlockSpec((tl, D_LRU), lambda i: (i, 0)),
        out_shape=jax.ShapeDtypeStruct((S, D_LRU), BF16),
        scratch_shapes=[
            pltpu.VMEM((tl + SUBLANE, D_LRU), F32),
            pltpu.VMEM((tl, D_LRU), F32),
            pltpu.VMEM((tl, D_LRU), F32),
            pltpu.VMEM((SUBLANE, D_LRU), F32),
        ],
        compiler_params=pltpu.CompilerParams(
            dimension_semantics=("arbitrary",), vmem_limit_bytes=VMEM_LIMIT),
        name="rglru",
    )(proj, proj, proj, conv_w, conv_b, w_gates, b_rg, b_ig, lam, b_merge)


def _attn_kernel(q_ref, k_ref, v_ref, sg_ref, mg_ref, bm_ref, o_ref, *, nsub, tq):
    g = pl.program_id(1)
    scale = SB_HEAD_DIM ** -0.5
    ri = lax.broadcasted_iota(jnp.int32, (tq, tq), 0)
    ci = lax.broadcasted_iota(jnp.int32, (tq, tq), 1)
    causal = ci < ri
    upper = jnp.where(ri > ci, 1.0, 0.0).astype(BF16)

    def step(q, kb, rem, acc, masked):
        s0 = pl.multiple_of(kb * tq, tq)
        k = k_ref[pl.ds(s0, tq), :]
        v = v_ref[pl.ds(s0, tq), :]
        z = lax.dot_general(q, k, (((1,), (1,)), ((), ())), preferred_element_type=F32) * scale
        sp = _softplus(z)
        lr = -sp
        if masked:
            lr = jnp.where(causal, lr, 0.0)
        hi = lr.astype(BF16)
        lo = (lr - hi.astype(F32)).astype(BF16)
        between = (jnp.dot(hi, upper, preferred_element_type=F32)
                   + jnp.dot(lo, upper, preferred_element_type=F32))
        w = jnp.exp((z - sp) + (rem + between))
        if masked:
            w = jnp.where(causal, w, 0.0)
        acc = acc + jnp.dot(w.astype(BF16), v, preferred_element_type=F32)
        rem = rem + jnp.sum(lr, axis=1, keepdims=True)
        return rem, acc

    for r in range(nsub):
        rows = slice(r * tq, (r + 1) * tq)
        diag = g * nsub + r
        q = q_ref[rows, :]
        rem0, acc0 = step(q, diag, jnp.zeros((tq, 1), F32), jnp.zeros((tq, SB_HEAD_DIM), F32), True)

        def cond(c):
            kb, rem_max, _, _ = c
            return jnp.logical_and(kb >= 0, rem_max >= EXIT_LOG)

        def body(c, q=q):
            kb, _, rem, acc = c
            rem, acc = step(q, kb, rem, acc, False)
            return kb - 1, jnp.max(rem), rem, acc

        _, _, _, acc = lax.while_loop(cond, body, (diag - 1, jnp.max(rem0), rem0, acc0))

        gate = (_silu(sg_ref[rows, :].astype(F32))
                * _sigmoid(mg_ref[rows, :].astype(F32) + bm_ref[...]))
        o_ref[rows, :] = (acc * gate).astype(BF16)


def _attn(proj, b_merge, nsub=4, tq=128):
    S = proj.shape[0]
    tg = nsub * tq
    hb = lambda off: off // SB_HEAD_DIM
    blk = lambda off: pl.BlockSpec((tg, SB_HEAD_DIM), lambda h, g, o=hb(off): (g, o + h))
    seq = lambda off: pl.BlockSpec((S, SB_HEAD_DIM), lambda h, g, o=hb(off): (0, o + h))
    return pl.pallas_call(
        functools.partial(_attn_kernel, nsub=nsub, tq=tq),
        grid=(SB_HEADS, S // tg),
        in_specs=[
            blk(OFF_Q), seq(OFF_K), seq(OFF_V), blk(OFF_SB_GATE), blk(OFF_MERGE_B),
            pl.BlockSpec((1, SB_HEAD_DIM), lambda h, g: (0, hb(D_LRU) + h)),
        ],
        out_specs=pl.BlockSpec((tg, SB_HEAD_DIM), lambda h, g: (g, h)),
        out_shape=jax.ShapeDtypeStruct((S, D_SB), BF16),
        compiler_params=pltpu.CompilerParams(
            dimension_semantics=("arbitrary", "arbitrary"), vmem_limit_bytes=VMEM_LIMIT),
        name="sb_attn",
    )(proj, proj, proj, proj, proj, b_merge)


def _outproj_kernel(x_ref, ya_ref, yb_ref, wa_ref, wb_ref, fg_ref, o_ref, *, final):
    acc = jnp.dot(ya_ref[...], wa_ref[...], preferred_element_type=F32)
    acc = acc + jnp.dot(yb_ref[...], wb_ref[...], preferred_element_type=F32)
    xr = x_ref[...] + acc
    if final:
        ms = jnp.mean(xr * xr, axis=-1, keepdims=True)
        xr = xr * lax.rsqrt(ms + EPS) * fg_ref[...]
    o_ref[...] = xr


def _outproj(x, y_a, y_b, w_a, w_b, final_g, final, tm=512):
    S = x.shape[0]
    const = lambda shape: pl.BlockSpec(shape, lambda i: (0, 0), pipeline_mode=pl.Buffered(1))
    return pl.pallas_call(
        functools.partial(_outproj_kernel, final=final),
        grid=(S // tm,),
        in_specs=[
            pl.BlockSpec((tm, D_MODEL), lambda i: (i, 0)),
            pl.BlockSpec((tm, D_LRU), lambda i: (i, 0)),
            pl.BlockSpec((tm, D_SB), lambda i: (i, 0)),
            const((D_LRU, D_MODEL)),
            const((D_SB, D_MODEL)),
            pl.BlockSpec((1, D_MODEL), lambda i: (0, 0)),
        ],
        out_specs=pl.BlockSpec((tm, D_MODEL), lambda i: (i, 0)),
        out_shape=jax.ShapeDtypeStruct((S, D_MODEL), F32),
        compiler_params=pltpu.CompilerParams(
            dimension_semantics=("arbitrary",), vmem_limit_bytes=VMEM_LIMIT),
        name="outproj",
    )(x, y_a, y_b, w_a, w_b, final_g)


def kernel(x, norm_g, w_in, conv_w, conv_b, w_rg, b_rg, w_ig, b_ig, lru_lambda, b_merge, w_out, final_g):
    B, S, _ = x.shape
    depth = norm_g.shape[0]
    outs = []
    for b in range(B):
        xb = x[b]
        for l in range(depth):
            w_in_l = w_in[l].astype(BF16)
            w_gates = jnp.concatenate([w_rg[l], w_ig[l]], axis=-1).astype(BF16)
            w_a = w_out[l, :D_LRU].astype(BF16)
            w_b = w_out[l, D_LRU:].astype(BF16)
            row = lambda v: v.reshape(1, -1)

            proj = _inproj(xb, row(norm_g[l]), w_in_l)
            y_a = _lru(proj, conv_w[l], row(conv_b[l]), w_gates, row(b_rg[l]), row(b_ig[l]),
                       row(lru_lambda[l]), row(b_merge[l]))
            y_b = _attn(proj, row(b_merge[l]))
            xb = _outproj(xb, y_a, y_b, w_a, w_b, row(final_g), final=(l == depth - 1))
        outs.append(xb)
    return outs[0][None] if B == 1 else jnp.stack(outs, axis=0)
```

```python
import functools

import jax
import jax.numpy as jnp
from jax import lax
from jax.experimental import pallas as pl
from jax.experimental.pallas import tpu as pltpu

D_MODEL = 2048
D_LRU = 2048
LRU_BLOCKS = 16
LRU_BLOCK_W = 128
CONV_W = 4
LRU_C = 8.0
SB_HEADS = 16
SB_HEAD_DIM = 128
D_SB = SB_HEADS * SB_HEAD_DIM
D_CAT = D_LRU + D_SB
N_IN = 2 * D_LRU + 4 * D_SB + D_CAT
EPS = 1e-6

OFF_LRU_X = 0
OFF_LRU_GATE = OFF_LRU_X + D_LRU
OFF_Q = OFF_LRU_GATE + D_LRU
OFF_K = OFF_Q + D_SB
OFF_V = OFF_K + D_SB
OFF_SB_GATE = OFF_V + D_SB
OFF_MERGE_A = OFF_SB_GATE + D_SB
OFF_MERGE_B = OFF_MERGE_A + D_LRU

LANE = 128
SUBLANE = 8
VMEM_LIMIT = 56 * 1024 * 1024

EXIT_LOG2 = -150.5
LOG2_E = 1.4426950408889634

F32 = jnp.float32
BF16 = jnp.bfloat16


def _softplus2(z2):
    return jnp.maximum(z2, 0.0) + jnp.log2(1.0 + jnp.exp2(-jnp.abs(z2)))


def _sigmoid(z):
    return 0.5 * jnp.tanh(0.5 * z) + 0.5


def _silu(z):
    return z * _sigmoid(z)


def _inproj_kernel(x_ref, g_ref, w_ref, o_ref, h_ref):
    @pl.when(pl.program_id(1) == 0)
    def _():
        x = x_ref[...]
        ms = jnp.mean(x * x, axis=-1, keepdims=True)
        h_ref[...] = (x * lax.rsqrt(ms + EPS) * g_ref[...]).astype(BF16)

    o_ref[...] = jnp.dot(h_ref[...], w_ref[...], preferred_element_type=F32).astype(BF16)


def _inproj(x, norm_g, w_bf16, tm=1024, tn=1024):
    S = x.shape[0]
    return pl.pallas_call(
        _inproj_kernel,
        grid=(S // tm, N_IN // tn),
        in_specs=[
            pl.BlockSpec((tm, D_MODEL), lambda i, j: (i, 0)),
            pl.BlockSpec((1, D_MODEL), lambda i, j: (0, 0)),
            pl.BlockSpec((D_MODEL, tn), lambda i, j: (0, j)),
        ],
        out_specs=pl.BlockSpec((tm, tn), lambda i, j: (i, j)),
        out_shape=jax.ShapeDtypeStruct((S, N_IN), BF16),
        scratch_shapes=[pltpu.VMEM((tm, D_MODEL), BF16)],
        compiler_params=pltpu.CompilerParams(
            dimension_semantics=("arbitrary", "arbitrary"), vmem_limit_bytes=VMEM_LIMIT),
        name="inproj",
    )(x, norm_g, w_bf16)


def _lru_kernel(px_ref, pg_ref, pm_ref, cw_ref, cb_ref, wg_ref, brg_ref, big_ref, lam_ref, bm_ref,
                y_ref, xs_ref, hs_ref, hc_ref, *, tl):
    i = pl.program_id(0)
    ng = tl // SUBLANE

    @pl.when(i == 0)
    def _():
        xs_ref[:, 0:SUBLANE, :] = jnp.zeros((LRU_BLOCKS, SUBLANE, LRU_BLOCK_W), F32)
        hc_ref[...] = jnp.zeros((SUBLANE, D_LRU), F32)

    lam = lam_ref[...]
    log_sig_lam = -(jnp.maximum(-lam, 0.0) + jnp.log1p(jnp.exp(-jnp.abs(lam))))
    sub = lax.broadcasted_iota(jnp.int32, (SUBLANE, LRU_BLOCK_W), 0)

    for c in range(LRU_BLOCKS):
        cs = slice(c * LRU_BLOCK_W, (c + 1) * LRU_BLOCK_W)
        xs_ref[c, SUBLANE:SUBLANE + tl, :] = px_ref[:, cs].astype(F32)
        pieces = []
        for g in range(ng):
            acc = cb_ref[:, cs]
            for tap in range(CONV_W):
                start = SUBLANE + g - (CONV_W - 1) + tap
                acc = acc + xs_ref[c, pl.ds(start, SUBLANE, stride=ng), :] * cw_ref[tap:tap + 1, cs]
            pieces.append(acc)
        xc = jnp.concatenate(pieces, axis=0)

        gates = jnp.dot(xc.astype(BF16), wg_ref[c], preferred_element_type=F32)
        r = _sigmoid(gates[:, :LRU_BLOCK_W] + brg_ref[:, cs])
        ig = _sigmoid(gates[:, LRU_BLOCK_W:] + big_ref[:, cs])
        log_a = (LRU_C * log_sig_lam[:, cs]) * r
        a = jnp.exp(log_a)
        t = jnp.tanh(log_a)
        om = -2.0 * t / (1.0 - t)
        root = jnp.where(om > 0.0, om * lax.rsqrt(om), 0.0)
        u = root * (ig * xc)

        rg = lambda v, g: v[g * SUBLANE:(g + 1) * SUBLANE]
        h = rg(u, 0)
        p = rg(a, 0)
        hs, ps = [h], [p]
        for g in range(1, ng):
            h = rg(a, g) * h + rg(u, g)
            p = rg(a, g) * p
            hs.append(h)
            ps.append(p)

        pa, hb = p, h
        for d in (1, 2, 4):
            keep = sub >= d
            hb = hb + pa * jnp.where(keep, pltpu.roll(hb, d, axis=0), 0.0)
            pa = pa * jnp.where(keep, pltpu.roll(pa, d, axis=0), 1.0)
        h_in = hc_ref[:, cs]
        h_end = pa * h_in + hb
        seg_in = jnp.where(sub >= 1, pltpu.roll(h_end, 1, axis=0), h_in)
        hc_ref[:, cs] = jnp.broadcast_to(h_end[SUBLANE - 1:SUBLANE], (SUBLANE, LRU_BLOCK_W))

        for g in range(ng):
            hs_ref[c, pl.ds(g, SUBLANE, stride=ng), :] = hs[g] + ps[g] * seg_in
        xs_ref[c, 0:SUBLANE, :] = xs_ref[c, tl:tl + SUBLANE, :]

        gate = (_silu(pg_ref[:, cs].astype(F32))
                * _sigmoid(pm_ref[:, cs].astype(F32) + bm_ref[:, cs]))
        y_ref[:, cs] = (hs_ref[c] * gate).astype(BF16)


def _lru(proj, conv_w, conv_b, w_gates, b_rg, b_ig, lam, b_merge, tl=256):
    S = proj.shape[0]
    row_spec = lambda off: pl.BlockSpec((tl, D_LRU), lambda i, o=off // D_LRU: (i, o))
    vec_spec = pl.BlockSpec((1, D_LRU), lambda i: (0, 0))
    return pl.pallas_call(
        functools.partial(_lru_kernel, tl=tl),
        grid=(S // tl,),
        in_specs=[
            row_spec(OFF_LRU_X), row_spec(OFF_LRU_GATE), row_spec(OFF_MERGE_A),
            pl.BlockSpec((CONV_W, D_LRU), lambda i: (0, 0)),
            vec_spec,
            pl.BlockSpec((LRU_BLOCKS, LRU_BLOCK_W, 2 * LRU_BLOCK_W), lambda i: (0, 0, 0)),
            vec_spec, vec_spec, vec_spec,
            pl.BlockSpec((1, D_LRU), lambda i: (0, 0)),
        ],
        out_specs=pl.BlockSpec((tl, D_LRU), lambda i: (i, 0)),
        out_shape=jax.ShapeDtypeStruct((S, D_LRU), BF16),
        scratch_shapes=[
            pltpu.VMEM((LRU_BLOCKS, tl + SUBLANE, LRU_BLOCK_W), F32),
            pltpu.VMEM((LRU_BLOCKS, tl, LRU_BLOCK_W), F32),
            pltpu.VMEM((SUBLANE, D_LRU), F32),
        ],
        compiler_params=pltpu.CompilerParams(
            dimension_semantics=("arbitrary",), vmem_limit_bytes=VMEM_LIMIT),
        name="rglru",
    )(proj, proj, proj, conv_w, conv_b, w_gates, b_rg, b_ig, lam, b_merge)


def _attn_kernel(q_ref, k_ref, v_ref, sg_ref, mg_ref, bm_ref, o_ref, rem_ref, acc_ref, *, nsub, tq, near):
    g = pl.program_id(1)
    scale = SB_HEAD_DIM ** -0.5 * LOG2_E
    ri = lax.broadcasted_iota(jnp.int32, (tq, tq), 0)
    ci = lax.broadcasted_iota(jnp.int32, (tq, tq), 1)
    causal = ci < ri
    tri_ones = jnp.concatenate(
        [jnp.where(ri > ci, 1.0, 0.0), jnp.ones((tq, tq), F32)], axis=1).astype(BF16)

    def block(q, kb, rem, masked):
        s0 = kb * tq if isinstance(kb, int) else pl.multiple_of(kb * tq, tq)
        k = k_ref[pl.ds(s0, tq), :]
        v = v_ref[pl.ds(s0, tq), :]
        z = lax.dot_general(q, k, (((1,), (1,)), ((), ())), preferred_element_type=F32) * scale
        sp = _softplus2(z)
        lr = -sp
        if masked:
            lr = jnp.where(causal, lr, 0.0)
        hi = lr.astype(BF16)
        lo = (lr - hi.astype(F32)).astype(BF16)
        sums = jnp.dot(jnp.concatenate([hi, lo], axis=0), tri_ones, preferred_element_type=F32)
        sums = sums[:tq] + sums[tq:]
        w = jnp.exp2((z - sp) + (rem + sums[:, :tq]))
        if masked:
            w = jnp.where(causal, w, 0.0)
        pv = jnp.dot(w.astype(BF16), v, preferred_element_type=F32)
        return pv, rem + sums[:, tq:]

    def gate(rows):
        return (_silu(sg_ref[rows, :].astype(F32))
                * _sigmoid(mg_ref[rows, :].astype(F32) + bm_ref[...]))

    def first_group():
        for r in range(nsub):
            rows = slice(r * tq, (r + 1) * tq)
            q = q_ref[rows, :]
            rem = jnp.zeros((tq, tq), F32)
            acc = jnp.zeros((tq, SB_HEAD_DIM), F32)
            for b in range(r + 1):
                pv, rem = block(q, r - b, rem, masked=(b == 0))
                acc = acc + pv
            o_ref[rows, :] = (acc * gate(rows)).astype(BF16)

    def near_field():
        win = near * tq
        wrows = [pl.ds(pl.multiple_of((g * nsub + r - (near - 1)) * tq, tq), win) for r in range(nsub)]
        neg_tri = -jnp.concatenate([tri_ones, tri_ones], axis=0)
        zs, halves = [], []
        for r in range(nsub):
            q = q_ref[r * tq:(r + 1) * tq, :]
            z = lax.dot_general(q, k_ref[wrows[r], :], (((1,), (1,)), ((), ())),
                                preferred_element_type=F32) * scale
            sp = _softplus2(z)
            sp = jnp.concatenate([sp[:, :win - tq], jnp.where(causal, sp[:, win - tq:], 0.0)], axis=1)
            zs.append(z - sp)
            hi = sp.astype(BF16)
            lo = (sp - hi.astype(F32)).astype(BF16)
            for b in range(near):
                cols = slice(b * tq, (b + 1) * tq)
                halves.append(jnp.concatenate([hi[:, cols], lo[:, cols]], axis=1))
        sums = jnp.dot(jnp.concatenate(halves, axis=0), neg_tri, preferred_element_type=F32)
        rem_max = None
        for r in range(nsub):
            rows = slice(r * tq, (r + 1) * tq)
            rem = None
            ws = [None] * near
            for b in reversed(range(near)):
                s = sums[(r * near + b) * tq:(r * near + b + 1) * tq]
                e = zs[r][:, b * tq:(b + 1) * tq] + s[:, :tq]
                w = jnp.exp2(e if rem is None else e + rem)
                if b == near - 1:
                    w = jnp.where(causal, w, 0.0)
                ws[b] = w.astype(BF16)
                rem = s[:, tq:] if rem is None else rem + s[:, tq:]
            acc = jnp.dot(jnp.concatenate(ws, axis=1), v_ref[wrows[r], :], preferred_element_type=F32)
            rem_ref[r] = rem
            acc_ref[r] = acc
            o_ref[rows, :] = (acc * gate(rows)).astype(BF16)
            rem_max = rem if rem_max is None else jnp.maximum(rem_max, rem)
        return jnp.max(rem_max)

    def far_field():
        for r in range(nsub):
            rows = slice(r * tq, (r + 1) * tq)

            def cond(c):
                kb, rem_max = c
                return jnp.logical_and(kb >= 0, rem_max >= EXIT_LOG2)

            def body(c, r=r, rows=rows):
                kb, _ = c
                pv, rem = block(q_ref[rows, :], kb, rem_ref[r], masked=False)
                acc_ref[r] = acc_ref[r] + pv
                rem_ref[r] = rem
                return kb - 1, jnp.max(rem)

            lax.while_loop(cond, body, (g * nsub + r - near, jnp.max(rem_ref[r])))
            o_ref[rows, :] = (acc_ref[r] * gate(rows)).astype(BF16)

    @pl.when(g == 0)
    def _():
        first_group()

    @pl.when(g > 0)
    def _():
        rem_max = near_field()

        @pl.when(rem_max >= EXIT_LOG2)
        def _():
            far_field()


def _attn(proj, b_merge, nsub=4, tq=128, near=3):
    S = proj.shape[0]
    tg = nsub * tq
    assert nsub >= near - 1, "from the second group on, the key window must start at a frame >= 0"
    hb = lambda off: off // SB_HEAD_DIM
    blk = lambda off: pl.BlockSpec((tg, SB_HEAD_DIM), lambda h, g, o=hb(off): (g, o + h))
    seq = lambda off: pl.BlockSpec((S, SB_HEAD_DIM), lambda h, g, o=hb(off): (0, o + h))
    return pl.pallas_call(
        functools.partial(_attn_kernel, nsub=nsub, tq=tq, near=near),
        grid=(SB_HEADS, S // tg),
        in_specs=[
            blk(OFF_Q), seq(OFF_K), seq(OFF_V), blk(OFF_SB_GATE), blk(OFF_MERGE_B),
            pl.BlockSpec((1, SB_HEAD_DIM), lambda h, g: (0, hb(D_LRU) + h)),
        ],
        out_specs=pl.BlockSpec((tg, SB_HEAD_DIM), lambda h, g: (g, h)),
        out_shape=jax.ShapeDtypeStruct((S, D_SB), BF16),
        scratch_shapes=[
            pltpu.VMEM((nsub, tq, tq), F32),
            pltpu.VMEM((nsub, tq, SB_HEAD_DIM), F32),
        ],
        compiler_params=pltpu.CompilerParams(
            dimension_semantics=("arbitrary", "arbitrary"), vmem_limit_bytes=VMEM_LIMIT),
        name="sb_attn",
    )(proj, proj, proj, proj, proj, b_merge)


def _outproj_kernel(x_ref, ya_ref, yb_ref, wa_ref, wb_ref, fg_ref, o_ref, *, final):
    acc = jnp.dot(ya_ref[...], wa_ref[...], preferred_element_type=F32)
    acc = acc + jnp.dot(yb_ref[...], wb_ref[...], preferred_element_type=F32)
    xr = x_ref[...] + acc
    if final:
        ms = jnp.mean(xr * xr, axis=-1, keepdims=True)
        xr = xr * lax.rsqrt(ms + EPS) * fg_ref[...]
    o_ref[...] = xr


def _outproj(x, y_a, y_b, w_a, w_b, final_g, final, tm=512):
    S = x.shape[0]
    const = lambda shape: pl.BlockSpec(shape, lambda i: (0, 0), pipeline_mode=pl.Buffered(1))
    return pl.pallas_call(
        functools.partial(_outproj_kernel, final=final),
        grid=(S // tm,),
        in_specs=[
            pl.BlockSpec((tm, D_MODEL), lambda i: (i, 0)),
            pl.BlockSpec((tm, D_LRU), lambda i: (i, 0)),
            pl.BlockSpec((tm, D_SB), lambda i: (i, 0)),
            const((D_LRU, D_MODEL)),
            const((D_SB, D_MODEL)),
            pl.BlockSpec((1, D_MODEL), lambda i: (0, 0)),
        ],
        out_specs=pl.BlockSpec((tm, D_MODEL), lambda i: (i, 0)),
        out_shape=jax.ShapeDtypeStruct((S, D_MODEL), F32),
        compiler_params=pltpu.CompilerParams(
            dimension_semantics=("arbitrary",), vmem_limit_bytes=VMEM_LIMIT),
        name="outproj",
    )(x, y_a, y_b, w_a, w_b, final_g)


def kernel(x, norm_g, w_in, conv_w, conv_b, w_rg, b_rg, w_ig, b_ig, lru_lambda, b_merge, w_out, final_g):
    B, S, _ = x.shape
    depth = norm_g.shape[0]
    outs = []
    for b in range(B):
        xb = x[b]
        for l in range(depth):
            w_in_l = w_in[l].astype(BF16)
            w_gates = jnp.concatenate([w_rg[l], w_ig[l]], axis=-1).astype(BF16)
            w_a = w_out[l, :D_LRU].astype(BF16)
            w_b = w_out[l, D_LRU:].astype(BF16)
            row = lambda v: v.reshape(1, -1)

            proj = _inproj(xb, row(norm_g[l]), w_in_l)
            y_a = _lru(proj, conv_w[l], row(conv_b[l]), w_gates, row(b_rg[l]), row(b_ig[l]),
                       row(lru_lambda[l]), row(b_merge[l]))
            y_b = _attn(proj, row(b_merge[l]))
            xb = _outproj(xb, y_a, y_b, w_a, w_b, row(final_g), final=(l == depth - 1))
        outs.append(xb)
    return outs[0][None] if B == 1 else jnp.stack(outs, axis=0)
```

```python
import functools

import jax
import jax.numpy as jnp
from jax import lax
from jax.experimental import pallas as pl
from jax.experimental.pallas import tpu as pltpu

D_MODEL = 2048
D_LRU = 2048
LRU_BLOCKS = 16
LRU_BLOCK_W = 128
CONV_W = 4
LRU_C = 8.0
SB_HEADS = 16
SB_HEAD_DIM = 128
D_SB = SB_HEADS * SB_HEAD_DIM
D_CAT = D_LRU + D_SB
N_IN = 2 * D_LRU + 4 * D_SB + D_CAT
EPS = 1e-6

OFF_LRU_X = 0
OFF_LRU_GATE = OFF_LRU_X + D_LRU
OFF_Q = OFF_LRU_GATE + D_LRU
OFF_K = OFF_Q + D_SB
OFF_V = OFF_K + D_SB
OFF_SB_GATE = OFF_V + D_SB
OFF_MERGE_A = OFF_SB_GATE + D_SB
OFF_MERGE_B = OFF_MERGE_A + D_LRU

LANE = 128
SUBLANE = 8
VMEM_LIMIT = 56 * 1024 * 1024

EXIT_LOG2 = -150.5
LOG2_E = 1.4426950408889634

F32 = jnp.float32
BF16 = jnp.bfloat16


def _softplus2(z2):
    return jnp.maximum(z2, 0.0) + jnp.log2(1.0 + jnp.exp2(-jnp.abs(z2)))


def _sigmoid(z):
    return 0.5 * jnp.tanh(0.5 * z) + 0.5


def _silu(z):
    return z * _sigmoid(z)


def _inproj_kernel(x_ref, g_ref, w_ref, o_ref, h_ref):
    @pl.when(pl.program_id(1) == 0)
    def _():
        x = x_ref[...]
        ms = jnp.mean(x * x, axis=-1, keepdims=True)
        h_ref[...] = (x * lax.rsqrt(ms + EPS) * g_ref[...]).astype(BF16)

    o_ref[...] = jnp.dot(h_ref[...], w_ref[...], preferred_element_type=F32).astype(BF16)


def _inproj(x, norm_g, w_bf16, tm=1024, tn=1024):
    S = x.shape[0]
    return pl.pallas_call(
        _inproj_kernel,
        grid=(S // tm, N_IN // tn),
        in_specs=[
            pl.BlockSpec((tm, D_MODEL), lambda i, j: (i, 0)),
            pl.BlockSpec((1, D_MODEL), lambda i, j: (0, 0)),
            pl.BlockSpec((D_MODEL, tn), lambda i, j: (0, j)),
        ],
        out_specs=pl.BlockSpec((tm, tn), lambda i, j: (i, j)),
        out_shape=jax.ShapeDtypeStruct((S, N_IN), BF16),
        scratch_shapes=[pltpu.VMEM((tm, D_MODEL), BF16)],
        compiler_params=pltpu.CompilerParams(
            dimension_semantics=("arbitrary", "arbitrary"), vmem_limit_bytes=VMEM_LIMIT),
        name="inproj",
    )(x, norm_g, w_bf16)


def _lru_kernel(px_ref, pg_ref, pm_ref, cw_ref, cb_ref, wg_ref, brg_ref, big_ref, lam_ref, bm_ref,
                y_ref, xs_ref, hs_ref, hc_ref, *, tl):
    i = pl.program_id(0)
    ng = tl // SUBLANE

    @pl.when(i == 0)
    def _():
        xs_ref[:, 0:SUBLANE, :] = jnp.zeros((LRU_BLOCKS, SUBLANE, LRU_BLOCK_W), F32)
        hc_ref[...] = jnp.zeros((SUBLANE, D_LRU), F32)

    lam = lam_ref[...]
    log_sig_lam = -(jnp.maximum(-lam, 0.0) + jnp.log1p(jnp.exp(-jnp.abs(lam))))
    half_c_lsl = (0.5 * LRU_C) * log_sig_lam
    sub = lax.broadcasted_iota(jnp.int32, (SUBLANE, LRU_BLOCK_W), 0)

    for c in range(LRU_BLOCKS):
        cs = slice(c * LRU_BLOCK_W, (c + 1) * LRU_BLOCK_W)
        xs_ref[c, SUBLANE:SUBLANE + tl, :] = px_ref[:, cs].astype(F32)
        pieces = []
        for g in range(ng):
            acc = cb_ref[:, cs]
            for tap in range(CONV_W):
                start = SUBLANE + g - (CONV_W - 1) + tap
                acc = acc + xs_ref[c, pl.ds(start, SUBLANE, stride=ng), :] * cw_ref[tap:tap + 1, cs]
            pieces.append(acc)
        xc = jnp.concatenate(pieces, axis=0)

        gates = jnp.dot(xc.astype(BF16), wg_ref[c], preferred_element_type=F32)
        r_t = jnp.tanh(gates[:, :LRU_BLOCK_W] + 0.5 * brg_ref[:, cs])
        i_t = jnp.tanh(gates[:, LRU_BLOCK_W:] + 0.5 * big_ref[:, cs])
        log_a = half_c_lsl[:, cs] * r_t + half_c_lsl[:, cs]
        a = jnp.exp(log_a)
        t = jnp.tanh(log_a)
        om = -0.5 * t / (1.0 - t)
        root = jnp.where(om > 0.0, om * lax.rsqrt(om), 0.0)
        u = root * ((i_t + 1.0) * xc)

        rg = lambda v, g: v[g * SUBLANE:(g + 1) * SUBLANE]
        h = rg(u, 0)
        p = rg(a, 0)
        hs, ps = [h], [p]
        for g in range(1, ng):
            h = rg(a, g) * h + rg(u, g)
            p = rg(a, g) * p
            hs.append(h)
            ps.append(p)

        pa, hb = p, h
        for d in (1, 2, 4):
            keep = sub >= d
            hb = hb + pa * jnp.where(keep, pltpu.roll(hb, d, axis=0), 0.0)
            pa = pa * jnp.where(keep, pltpu.roll(pa, d, axis=0), 1.0)
        h_in = hc_ref[:, cs]
        h_end = pa * h_in + hb
        seg_in = jnp.where(sub >= 1, pltpu.roll(h_end, 1, axis=0), h_in)
        hc_ref[:, cs] = jnp.broadcast_to(h_end[SUBLANE - 1:SUBLANE], (SUBLANE, LRU_BLOCK_W))

        for g in range(ng):
            hs_ref[c, pl.ds(g, SUBLANE, stride=ng), :] = hs[g] + ps[g] * seg_in
        xs_ref[c, 0:SUBLANE, :] = xs_ref[c, tl:tl + SUBLANE, :]

        gate = (_silu(pg_ref[:, cs].astype(F32))
                * _sigmoid(pm_ref[:, cs].astype(F32) + bm_ref[:, cs]))
        y_ref[:, cs] = (hs_ref[c] * gate).astype(BF16)


def _lru(proj, conv_w, conv_b, w_gates, b_rg, b_ig, lam, b_merge, tl=256):
    S = proj.shape[0]
    row_spec = lambda off: pl.BlockSpec((tl, D_LRU), lambda i, o=off // D_LRU: (i, o))
    vec_spec = pl.BlockSpec((1, D_LRU), lambda i: (0, 0))
    return pl.pallas_call(
        functools.partial(_lru_kernel, tl=tl),
        grid=(S // tl,),
        in_specs=[
            row_spec(OFF_LRU_X), row_spec(OFF_LRU_GATE), row_spec(OFF_MERGE_A),
            pl.BlockSpec((CONV_W, D_LRU), lambda i: (0, 0)),
            vec_spec,
            pl.BlockSpec((LRU_BLOCKS, LRU_BLOCK_W, 2 * LRU_BLOCK_W), lambda i: (0, 0, 0)),
            vec_spec, vec_spec, vec_spec,
            pl.BlockSpec((1, D_LRU), lambda i: (0, 0)),
        ],
        out_specs=pl.BlockSpec((tl, D_LRU), lambda i: (i, 0)),
        out_shape=jax.ShapeDtypeStruct((S, D_LRU), BF16),
        scratch_shapes=[
            pltpu.VMEM((LRU_BLOCKS, tl + SUBLANE, LRU_BLOCK_W), F32),
            pltpu.VMEM((LRU_BLOCKS, tl, LRU_BLOCK_W), F32),
            pltpu.VMEM((SUBLANE, D_LRU), F32),
        ],
        compiler_params=pltpu.CompilerParams(
            dimension_semantics=("arbitrary",), vmem_limit_bytes=VMEM_LIMIT),
        name="rglru",
    )(proj, proj, proj, conv_w, conv_b, w_gates, b_rg, b_ig, lam, b_merge)


_QK_DIMS = (((1,), (1,)), ((), ()))


def _attn_kernel(q_ref, kp_ref, kc_ref, vp_ref, vc_ref, sg_ref, mg_ref, bm_ref, proj_hbm, o_ref,
                 rem_ref, acc_ref, q_buf, k_buf, v_buf, sem, *, nsub, tq, hpg):
    g = pl.program_id(0)
    near = nsub + 1
    win = near * tq
    scale = SB_HEAD_DIM ** -0.5 * LOG2_E
    ri = lax.broadcasted_iota(jnp.int32, (tq, tq), 0)
    ci = lax.broadcasted_iota(jnp.int32, (tq, tq), 1)
    causal = ci < ri
    neg_tri = -jnp.concatenate(
        [jnp.where(ri > ci, 1.0, 0.0), jnp.ones((tq, tq), F32)], axis=1).astype(BF16)
    neg_tri = jnp.concatenate([neg_tri, neg_tri], axis=0)
    prev_bias = jnp.where(g > 0, 0.0, -1e30)

    def split(sp):
        hi = sp.astype(BF16)
        lo = (sp - hi.astype(F32)).astype(BF16)
        return hi, lo

    def gate(rows, cols):
        return (_silu(sg_ref[rows, cols].astype(F32))
                * _sigmoid(mg_ref[rows, cols].astype(F32) + bm_ref[:, cols]))

    def near_field(heads):
        units = [(h, r) for h in heads for r in range(nsub)]
        zs, halves = [], []
        for h, r in units:
            cols = slice(h * SB_HEAD_DIM, (h + 1) * SB_HEAD_DIM)
            q = q_ref[r * tq:(r + 1) * tq, cols]
            zp = lax.dot_general(q, kp_ref[r * tq:, cols], _QK_DIMS, preferred_element_type=F32)
            zc = lax.dot_general(q, kc_ref[:(r + 1) * tq, cols], _QK_DIMS, preferred_element_type=F32)
            z = jnp.concatenate([zp * scale + prev_bias, zc * scale], axis=1)
            sp = _softplus2(z)
            sp = jnp.concatenate([sp[:, :win - tq], jnp.where(causal, sp[:, win - tq:], 0.0)], axis=1)
            zs.append(z - sp)
            hi, lo = split(sp)
            for b in range(near):
                bc = slice(b * tq, (b + 1) * tq)
                halves.append(jnp.concatenate([hi[:, bc], lo[:, bc]], axis=1))
        sums = jnp.dot(jnp.concatenate(halves, axis=0), neg_tri, preferred_element_type=F32)
        rem_max = None
        for u, (h, r) in enumerate(units):
            cols = slice(h * SB_HEAD_DIM, (h + 1) * SB_HEAD_DIM)
            rows = slice(r * tq, (r + 1) * tq)
            rem = None
            ws = [None] * near
            for b in reversed(range(near)):
                s = sums[(u * near + b) * tq:(u * near + b + 1) * tq]
                e = zs[u][:, b * tq:(b + 1) * tq] + s[:, :tq]
                w = jnp.exp2(e if rem is None else e + rem)
                if b == near - 1:
                    w = jnp.where(causal, w, 0.0)
                ws[b] = w.astype(BF16)
                rem = s[:, tq:] if rem is None else rem + s[:, tq:]
            w = jnp.concatenate(ws, axis=1)
            n_prev = (nsub - r) * tq
            acc = (jnp.dot(w[:, :n_prev], vp_ref[r * tq:, cols], preferred_element_type=F32)
                   + jnp.dot(w[:, n_prev:], vc_ref[:(r + 1) * tq, cols], preferred_element_type=F32))
            rem_ref[h * nsub + r] = rem
            acc_ref[h * nsub + r] = acc
            o_ref[rows, cols] = (acc * gate(rows, cols)).astype(BF16)
            rem_max = rem if rem_max is None else jnp.maximum(rem_max, rem)
        return rem_max

    def fetch(dst, slot, row0, col0):
        cp = pltpu.make_async_copy(
            proj_hbm.at[pl.ds(row0, tq), pl.ds(col0, SB_HEAD_DIM)], dst, sem.at[slot])
        cp.start()
        return cp

    def far_field():
        def unit(idx, carry):
            h = idx // nsub
            r = idx - h * nsub
            kb0 = g * nsub + r - near
            rem_max0 = jnp.max(rem_ref[idx])

            @pl.when(jnp.logical_and(kb0 >= 0, rem_max0 >= EXIT_LOG2))
            def _():
                hcol = h * SB_HEAD_DIM
                fetch(q_buf, 0, pl.multiple_of((g * nsub + r) * tq, tq),
                      pl.multiple_of(OFF_Q + hcol, SB_HEAD_DIM)).wait()

                def cond(c):
                    kb, rem_max = c
                    return jnp.logical_and(kb >= 0, rem_max >= EXIT_LOG2)

                def body(c):
                    kb, _ = c
                    row0 = pl.multiple_of(kb * tq, tq)
                    ck = fetch(k_buf, 1, row0, pl.multiple_of(OFF_K + hcol, SB_HEAD_DIM))
                    cv = fetch(v_buf, 2, row0, pl.multiple_of(OFF_V + hcol, SB_HEAD_DIM))
                    ck.wait()
                    cv.wait()
                    z = lax.dot_general(q_buf[...], k_buf[...], _QK_DIMS, preferred_element_type=F32) * scale
                    sp = _softplus2(z)
                    hi, lo = split(sp)
                    sums = jnp.dot(jnp.concatenate([hi, lo], axis=1), neg_tri, preferred_element_type=F32)
                    rem = rem_ref[idx]
                    w = jnp.exp2((z - sp) + (rem + sums[:, :tq]))
                    acc_ref[idx] = acc_ref[idx] + jnp.dot(w.astype(BF16), v_buf[...], preferred_element_type=F32)
                    rem = rem + sums[:, tq:]
                    rem_ref[idx] = rem
                    return kb - 1, jnp.max(rem)

                lax.while_loop(cond, body, (kb0, rem_max0))

            return carry

        lax.fori_loop(0, SB_HEADS * nsub, unit, 0)
        for h in range(SB_HEADS):
            cols = slice(h * SB_HEAD_DIM, (h + 1) * SB_HEAD_DIM)
            for r in range(nsub):
                rows = slice(r * tq, (r + 1) * tq)
                o_ref[rows, cols] = (acc_ref[h * nsub + r] * gate(rows, cols)).astype(BF16)

    rem_max = None
    for h0 in range(0, SB_HEADS, hpg):
        m = near_field(range(h0, h0 + hpg))
        rem_max = m if rem_max is None else jnp.maximum(rem_max, m)

    @pl.when(jnp.max(rem_max) >= EXIT_LOG2)
    def _():
        far_field()


def _attn(proj, b_merge, nsub=2, tq=128, hpg=2):
    S = proj.shape[0]
    tg = nsub * tq
    cb = lambda off: off // D_SB
    cur = lambda off: pl.BlockSpec((tg, D_SB), lambda g, o=cb(off): (g, o))
    prev = lambda off: pl.BlockSpec((tg, D_SB), lambda g, o=cb(off): (jnp.maximum(g - 1, 0), o))
    return pl.pallas_call(
        functools.partial(_attn_kernel, nsub=nsub, tq=tq, hpg=hpg),
        grid=(S // tg,),
        in_specs=[
            cur(OFF_Q), prev(OFF_K), cur(OFF_K), prev(OFF_V), cur(OFF_V),
            cur(OFF_SB_GATE), cur(OFF_MERGE_B),
            pl.BlockSpec((1, D_SB), lambda g: (0, cb(D_LRU))),
            pl.BlockSpec(memory_space=pl.ANY),
        ],
        out_specs=pl.BlockSpec((tg, D_SB), lambda g: (g, 0)),
        out_shape=jax.ShapeDtypeStruct((S, D_SB), BF16),
        scratch_shapes=[
            pltpu.VMEM((SB_HEADS * nsub, tq, tq), F32),
            pltpu.VMEM((SB_HEADS * nsub, tq, SB_HEAD_DIM), F32),
            pltpu.VMEM((tq, SB_HEAD_DIM), BF16),
            pltpu.VMEM((tq, SB_HEAD_DIM), BF16),
            pltpu.VMEM((tq, SB_HEAD_DIM), BF16),
            pltpu.SemaphoreType.DMA((3,)),
        ],
        compiler_params=pltpu.CompilerParams(
            dimension_semantics=("arbitrary",), vmem_limit_bytes=VMEM_LIMIT),
        name="sb_attn",
    )(proj, proj, proj, proj, proj, proj, proj, b_merge, proj)


def _outproj_kernel(x_ref, ya_ref, yb_ref, wa_ref, wb_ref, fg_ref, o_ref, *, final):
    acc = jnp.dot(ya_ref[...], wa_ref[...], preferred_element_type=F32)
    acc = acc + jnp.dot(yb_ref[...], wb_ref[...], preferred_element_type=F32)
    xr = x_ref[...] + acc
    if final:
        ms = jnp.mean(xr * xr, axis=-1, keepdims=True)
        xr = xr * lax.rsqrt(ms + EPS) * fg_ref[...]
    o_ref[...] = xr


def _outproj(x, y_a, y_b, w_a, w_b, final_g, final, tm=512):
    S = x.shape[0]
    const = lambda shape: pl.BlockSpec(shape, lambda i: (0, 0), pipeline_mode=pl.Buffered(1))
    return pl.pallas_call(
        functools.partial(_outproj_kernel, final=final),
        grid=(S // tm,),
        in_specs=[
            pl.BlockSpec((tm, D_MODEL), lambda i: (i, 0)),
            pl.BlockSpec((tm, D_LRU), lambda i: (i, 0)),
            pl.BlockSpec((tm, D_SB), lambda i: (i, 0)),
            const((D_LRU, D_MODEL)),
            const((D_SB, D_MODEL)),
            pl.BlockSpec((1, D_MODEL), lambda i: (0, 0)),
        ],
        out_specs=pl.BlockSpec((tm, D_MODEL), lambda i: (i, 0)),
        out_shape=jax.ShapeDtypeStruct((S, D_MODEL), F32),
        compiler_params=pltpu.CompilerParams(
            dimension_semantics=("arbitrary",), vmem_limit_bytes=VMEM_LIMIT),
        name="outproj",
    )(x, y_a, y_b, w_a, w_b, final_g)


def kernel(x, norm_g, w_in, conv_w, conv_b, w_rg, b_rg, w_ig, b_ig, lru_lambda, b_merge, w_out, final_g):
    B, S, _ = x.shape
    depth = norm_g.shape[0]
    outs = []
    for b in range(B):
        xb = x[b]
        for l in range(depth):
            w_in_l = w_in[l].astype(BF16)
            w_gates = (0.5 * jnp.concatenate([w_rg[l], w_ig[l]], axis=-1)).astype(BF16)
            w_a = w_out[l, :D_LRU].astype(BF16)
            w_b = w_out[l, D_LRU:].astype(BF16)
            row = lambda v: v.reshape(1, -1)

            proj = _inproj(xb, row(norm_g[l]), w_in_l)
            y_a = _lru(proj, conv_w[l], row(conv_b[l]), w_gates, row(b_rg[l]), row(b_ig[l]),
                       row(lru_lambda[l]), row(b_merge[l]))
            y_b = _attn(proj, row(b_merge[l]))
            xb = _outproj(xb, y_a, y_b, w_a, w_b, row(final_g), final=(l == depth - 1))
        outs.append(xb)
    return outs[0][None] if B == 1 else jnp.stack(outs, axis=0)
```

```python
import functools

import jax
import jax.numpy as jnp
from jax import lax
from jax.experimental import pallas as pl
from jax.experimental.pallas import tpu as pltpu

D_MODEL = 2048
D_LRU = 2048
LRU_BLOCKS = 16
LRU_BLOCK_W = 128
CONV_W = 4
LRU_C = 8.0
SB_HEADS = 16
SB_HEAD_DIM = 128
D_SB = SB_HEADS * SB_HEAD_DIM
D_CAT = D_LRU + D_SB
N_IN = 2 * D_LRU + 4 * D_SB + D_CAT
EPS = 1e-6

OFF_LRU_X = 0
OFF_LRU_GATE = OFF_LRU_X + D_LRU
OFF_Q = OFF_LRU_GATE + D_LRU
OFF_K = OFF_Q + D_SB
OFF_V = OFF_K + D_SB
OFF_SB_GATE = OFF_V + D_SB
OFF_MERGE_A = OFF_SB_GATE + D_SB
OFF_MERGE_B = OFF_MERGE_A + D_LRU

LANE = 128
SUBLANE = 8
VMEM_LIMIT = 56 * 1024 * 1024

EXIT_LOG2 = -150.5
LOG2_E = 1.4426950408889634

F32 = jnp.float32
BF16 = jnp.bfloat16


def _softplus2(z2):
    return jnp.maximum(z2, 0.0) + jnp.log2(1.0 + jnp.exp2(-jnp.abs(z2)))


def _sigmoid(z):
    return 0.5 * jnp.tanh(0.5 * z) + 0.5


def _silu(z):
    return z * _sigmoid(z)


def _inproj_kernel(x_ref, g_ref, w_ref, o_ref, h_ref):
    @pl.when(pl.program_id(1) == 0)
    def _():
        x = x_ref[...]
        ms = jnp.mean(x * x, axis=-1, keepdims=True)
        h_ref[...] = (x * lax.rsqrt(ms + EPS) * g_ref[...]).astype(BF16)

    o_ref[...] = jnp.dot(h_ref[...], w_ref[...], preferred_element_type=F32).astype(BF16)


def _inproj(x, norm_g, w_bf16, tm=1024, tn=1024):
    S = x.shape[0]
    return pl.pallas_call(
        _inproj_kernel,
        grid=(S // tm, N_IN // tn),
        in_specs=[
            pl.BlockSpec((tm, D_MODEL), lambda i, j: (i, 0)),
            pl.BlockSpec((1, D_MODEL), lambda i, j: (0, 0)),
            pl.BlockSpec((D_MODEL, tn), lambda i, j: (0, j)),
        ],
        out_specs=pl.BlockSpec((tm, tn), lambda i, j: (i, j)),
        out_shape=jax.ShapeDtypeStruct((S, N_IN), BF16),
        scratch_shapes=[pltpu.VMEM((tm, D_MODEL), BF16)],
        compiler_params=pltpu.CompilerParams(
            dimension_semantics=("arbitrary", "arbitrary"), vmem_limit_bytes=VMEM_LIMIT),
        name="inproj",
    )(x, norm_g, w_bf16)


def _lru_kernel(px_ref, pg_ref, pm_ref, cw_ref, cb_ref, wg_ref, brg_ref, big_ref, lam_ref, bm_ref,
                y_ref, xs_ref, hs_ref, hc_ref, *, tl):
    i = pl.program_id(0)
    ng = tl // SUBLANE

    @pl.when(i == 0)
    def _():
        xs_ref[:, 0:SUBLANE, :] = jnp.zeros((LRU_BLOCKS, SUBLANE, LRU_BLOCK_W), F32)
        hc_ref[...] = jnp.zeros((SUBLANE, D_LRU), F32)

    lam = lam_ref[...]
    log_sig_lam = -(jnp.maximum(-lam, 0.0) + jnp.log1p(jnp.exp(-jnp.abs(lam))))
    half_c_lsl = (0.5 * LRU_C) * log_sig_lam
    sub = lax.broadcasted_iota(jnp.int32, (SUBLANE, LRU_BLOCK_W), 0)

    for c in range(LRU_BLOCKS):
        cs = slice(c * LRU_BLOCK_W, (c + 1) * LRU_BLOCK_W)
        xs_ref[c, SUBLANE:SUBLANE + tl, :] = px_ref[:, cs].astype(F32)
        pieces = []
        for g in range(ng):
            acc = cb_ref[:, cs]
            for tap in range(CONV_W):
                start = SUBLANE + g - (CONV_W - 1) + tap
                acc = acc + xs_ref[c, pl.ds(start, SUBLANE, stride=ng), :] * cw_ref[tap:tap + 1, cs]
            pieces.append(acc)
        xc = jnp.concatenate(pieces, axis=0)

        gates = jnp.dot(xc.astype(BF16), wg_ref[c], preferred_element_type=F32)
        r_t = jnp.tanh(gates[:, :LRU_BLOCK_W] + 0.5 * brg_ref[:, cs])
        i_t = jnp.tanh(gates[:, LRU_BLOCK_W:] + 0.5 * big_ref[:, cs])
        log_a = half_c_lsl[:, cs] * r_t + half_c_lsl[:, cs]
        a = jnp.exp(log_a)
        t = jnp.tanh(log_a)
        om = -0.5 * t / (1.0 - t)
        root = jnp.where(om > 0.0, om * lax.rsqrt(om), 0.0)
        u = root * ((i_t + 1.0) * xc)

        rg = lambda v, g: v[g * SUBLANE:(g + 1) * SUBLANE]
        h = rg(u, 0)
        p = rg(a, 0)
        hs, ps = [h], [p]
        for g in range(1, ng):
            h = rg(a, g) * h + rg(u, g)
            p = rg(a, g) * p
            hs.append(h)
            ps.append(p)

        pa, hb = p, h
        for d in (1, 2, 4):
            keep = sub >= d
            hb = hb + pa * jnp.where(keep, pltpu.roll(hb, d, axis=0), 0.0)
            pa = pa * jnp.where(keep, pltpu.roll(pa, d, axis=0), 1.0)
        h_in = hc_ref[:, cs]
        h_end = pa * h_in + hb
        seg_in = jnp.where(sub >= 1, pltpu.roll(h_end, 1, axis=0), h_in)
        hc_ref[:, cs] = jnp.broadcast_to(h_end[SUBLANE - 1:SUBLANE], (SUBLANE, LRU_BLOCK_W))

        for g in range(ng):
            hs_ref[c, pl.ds(g, SUBLANE, stride=ng), :] = hs[g] + ps[g] * seg_in
        xs_ref[c, 0:SUBLANE, :] = xs_ref[c, tl:tl + SUBLANE, :]

        gate = (_silu(pg_ref[:, cs].astype(F32))
                * _sigmoid(pm_ref[:, cs].astype(F32) + bm_ref[:, cs]))
        y_ref[:, cs] = (hs_ref[c] * gate).astype(BF16)


def _lru(proj, conv_w, conv_b, w_gates, b_rg, b_ig, lam, b_merge, tl=256):
    S = proj.shape[0]
    row_spec = lambda off: pl.BlockSpec((tl, D_LRU), lambda i, o=off // D_LRU: (i, o))
    vec_spec = pl.BlockSpec((1, D_LRU), lambda i: (0, 0))
    return pl.pallas_call(
        functools.partial(_lru_kernel, tl=tl),
        grid=(S // tl,),
        in_specs=[
            row_spec(OFF_LRU_X), row_spec(OFF_LRU_GATE), row_spec(OFF_MERGE_A),
            pl.BlockSpec((CONV_W, D_LRU), lambda i: (0, 0)),
            vec_spec,
            pl.BlockSpec((LRU_BLOCKS, LRU_BLOCK_W, 2 * LRU_BLOCK_W), lambda i: (0, 0, 0)),
            vec_spec, vec_spec, vec_spec,
            pl.BlockSpec((1, D_LRU), lambda i: (0, 0)),
        ],
        out_specs=pl.BlockSpec((tl, D_LRU), lambda i: (i, 0)),
        out_shape=jax.ShapeDtypeStruct((S, D_LRU), BF16),
        scratch_shapes=[
            pltpu.VMEM((LRU_BLOCKS, tl + SUBLANE, LRU_BLOCK_W), F32),
            pltpu.VMEM((LRU_BLOCKS, tl, LRU_BLOCK_W), F32),
            pltpu.VMEM((SUBLANE, D_LRU), F32),
        ],
        compiler_params=pltpu.CompilerParams(
            dimension_semantics=("arbitrary",), vmem_limit_bytes=VMEM_LIMIT),
        name="rglru",
    )(proj, proj, proj, conv_w, conv_b, w_gates, b_rg, b_ig, lam, b_merge)


_QK_DIMS = (((1,), (1,)), ((), ()))


def _attn_out_kernel(q_ref, kp_ref, kc_ref, vp_ref, vc_ref, sg_ref, mg_ref, bm_ref, proj_hbm,
                     x_ref, ya_ref, wa_ref, wb_ref, fg_ref, o_ref,
                     yb_ref, rem_ref, acc_ref, q_buf, k_buf, v_buf, sem, *, nsub, tq, hpg, ngroups, final):
    t = pl.program_id(0)
    g = jnp.minimum(t, ngroups - 1)
    slot = lax.rem(t, 2)
    near = nsub + 1
    win = near * tq
    scale = SB_HEAD_DIM ** -0.5 * LOG2_E
    ri = lax.broadcasted_iota(jnp.int32, (tq, tq), 0)
    ci = lax.broadcasted_iota(jnp.int32, (tq, tq), 1)
    causal = ci < ri
    neg_tri = -jnp.concatenate(
        [jnp.where(ri > ci, 1.0, 0.0), jnp.ones((tq, tq), F32)], axis=1).astype(BF16)
    neg_tri = jnp.concatenate([neg_tri, neg_tri], axis=0)
    prev_bias = jnp.where(g > 0, 0.0, -1e30)

    def split(sp):
        hi = sp.astype(BF16)
        lo = (sp - hi.astype(F32)).astype(BF16)
        return hi, lo

    def gate(rows, cols):
        return (_silu(sg_ref[rows, cols].astype(F32))
                * _sigmoid(mg_ref[rows, cols].astype(F32) + bm_ref[:, cols]))

    def near_field(heads):
        units = [(h, r) for h in heads for r in range(nsub)]
        zs, halves = [], []
        for h, r in units:
            cols = slice(h * SB_HEAD_DIM, (h + 1) * SB_HEAD_DIM)
            q = q_ref[r * tq:(r + 1) * tq, cols]
            zp = lax.dot_general(q, kp_ref[r * tq:, cols], _QK_DIMS, preferred_element_type=F32)
            zc = lax.dot_general(q, kc_ref[:(r + 1) * tq, cols], _QK_DIMS, preferred_element_type=F32)
            z = jnp.concatenate([zp * scale + prev_bias, zc * scale], axis=1)
            sp = _softplus2(z)
            sp = jnp.concatenate([sp[:, :win - tq], jnp.where(causal, sp[:, win - tq:], 0.0)], axis=1)
            zs.append(z - sp)
            hi, lo = split(sp)
            for b in range(near):
                bc = slice(b * tq, (b + 1) * tq)
                halves.append(jnp.concatenate([hi[:, bc], lo[:, bc]], axis=1))
        sums = jnp.dot(jnp.concatenate(halves, axis=0), neg_tri, preferred_element_type=F32)
        rem_max = None
        for u, (h, r) in enumerate(units):
            cols = slice(h * SB_HEAD_DIM, (h + 1) * SB_HEAD_DIM)
            rows = slice(r * tq, (r + 1) * tq)
            rem = None
            ws = [None] * near
            for b in reversed(range(near)):
                s = sums[(u * near + b) * tq:(u * near + b + 1) * tq]
                e = zs[u][:, b * tq:(b + 1) * tq] + s[:, :tq]
                w = jnp.exp2(e if rem is None else e + rem)
                if b == near - 1:
                    w = jnp.where(causal, w, 0.0)
                ws[b] = w.astype(BF16)
                rem = s[:, tq:] if rem is None else rem + s[:, tq:]
            w = jnp.concatenate(ws, axis=1)
            n_prev = (nsub - r) * tq
            acc = (jnp.dot(w[:, :n_prev], vp_ref[r * tq:, cols], preferred_element_type=F32)
                   + jnp.dot(w[:, n_prev:], vc_ref[:(r + 1) * tq, cols], preferred_element_type=F32))
            rem_ref[h * nsub + r] = rem
            acc_ref[h * nsub + r] = acc
            yb_ref[slot, rows, cols] = (acc * gate(rows, cols)).astype(BF16)
            rem_max = rem if rem_max is None else jnp.maximum(rem_max, rem)
        return rem_max

    def fetch(dst, slot, row0, col0):
        cp = pltpu.make_async_copy(
            proj_hbm.at[pl.ds(row0, tq), pl.ds(col0, SB_HEAD_DIM)], dst, sem.at[slot])
        cp.start()
        return cp

    def far_field():
        def unit(idx, carry):
            h = idx // nsub
            r = idx - h * nsub
            kb0 = g * nsub + r - near
            rem_max0 = jnp.max(rem_ref[idx])

            @pl.when(jnp.logical_and(kb0 >= 0, rem_max0 >= EXIT_LOG2))
            def _():
                hcol = h * SB_HEAD_DIM
                fetch(q_buf, 0, pl.multiple_of((g * nsub + r) * tq, tq),
                      pl.multiple_of(OFF_Q + hcol, SB_HEAD_DIM)).wait()

                def cond(c):
                    kb, rem_max = c
                    return jnp.logical_and(kb >= 0, rem_max >= EXIT_LOG2)

                def body(c):
                    kb, _ = c
                    row0 = pl.multiple_of(kb * tq, tq)
                    ck = fetch(k_buf, 1, row0, pl.multiple_of(OFF_K + hcol, SB_HEAD_DIM))
                    cv = fetch(v_buf, 2, row0, pl.multiple_of(OFF_V + hcol, SB_HEAD_DIM))
                    ck.wait()
                    cv.wait()
                    z = lax.dot_general(q_buf[...], k_buf[...], _QK_DIMS, preferred_element_type=F32) * scale
                    sp = _softplus2(z)
                    hi, lo = split(sp)
                    sums = jnp.dot(jnp.concatenate([hi, lo], axis=1), neg_tri, preferred_element_type=F32)
                    rem = rem_ref[idx]
                    w = jnp.exp2((z - sp) + (rem + sums[:, :tq]))
                    acc_ref[idx] = acc_ref[idx] + jnp.dot(w.astype(BF16), v_buf[...], preferred_element_type=F32)
                    rem = rem + sums[:, tq:]
                    rem_ref[idx] = rem
                    return kb - 1, jnp.max(rem)

                lax.while_loop(cond, body, (kb0, rem_max0))

            return carry

        lax.fori_loop(0, SB_HEADS * nsub, unit, 0)
        for h in range(SB_HEADS):
            cols = slice(h * SB_HEAD_DIM, (h + 1) * SB_HEAD_DIM)
            for r in range(nsub):
                rows = slice(r * tq, (r + 1) * tq)
                yb_ref[slot, rows, cols] = (acc_ref[h * nsub + r] * gate(rows, cols)).astype(BF16)

    nchunks = SB_HEADS // hpg
    cn = D_MODEL // nchunks

    def outproj_chunk(i):
        cc = slice(i * cn, (i + 1) * cn)
        acc = jnp.dot(ya_ref[...], wa_ref[:, cc], preferred_element_type=F32)
        acc = acc + jnp.dot(yb_ref[1 - slot], wb_ref[:, cc], preferred_element_type=F32)
        xr = x_ref[:, cc] + acc
        o_ref[:, cc] = xr
        return jnp.sum(xr * xr, axis=1, keepdims=True)

    @pl.when(t == 0)
    def _():
        yb_ref[1] = jnp.zeros(yb_ref.shape[1:], BF16)

    rem_max = None
    ssq = None
    for i, h0 in enumerate(range(0, SB_HEADS, hpg)):
        m = near_field(range(h0, h0 + hpg))
        rem_max = m if rem_max is None else jnp.maximum(rem_max, m)
        s = outproj_chunk(i)
        ssq = s if ssq is None else ssq + s
    if final:
        o_ref[...] = o_ref[...] * lax.rsqrt(ssq * (1.0 / D_MODEL) + EPS) * fg_ref[...]

    @pl.when(jnp.max(rem_max) >= EXIT_LOG2)
    def _():
        far_field()


def _attn_outproj(proj, b_merge, x, y_a, w_a, w_b, final_g, final, nsub=2, tq=128, hpg=2):
    S = proj.shape[0]
    tg = nsub * tq
    ngroups = S // tg
    cb = lambda off: off // D_SB
    att = lambda t: jnp.minimum(t, ngroups - 1)
    out = lambda t: jnp.maximum(t - 1, 0)
    cur = lambda off: pl.BlockSpec((tg, D_SB), lambda t, o=cb(off): (att(t), o))
    prev = lambda off: pl.BlockSpec((tg, D_SB), lambda t, o=cb(off): (jnp.maximum(att(t) - 1, 0), o))
    const = lambda shape: pl.BlockSpec(shape, lambda t: (0, 0), pipeline_mode=pl.Buffered(1))
    return pl.pallas_call(
        functools.partial(_attn_out_kernel, nsub=nsub, tq=tq, hpg=hpg, ngroups=ngroups, final=final),
        grid=(ngroups + 1,),
        in_specs=[
            cur(OFF_Q), prev(OFF_K), cur(OFF_K), prev(OFF_V), cur(OFF_V),
            cur(OFF_SB_GATE), cur(OFF_MERGE_B),
            pl.BlockSpec((1, D_SB), lambda t: (0, cb(D_LRU))),
            pl.BlockSpec(memory_space=pl.ANY),
            pl.BlockSpec((tg, D_MODEL), lambda t: (out(t), 0)),
            pl.BlockSpec((tg, D_LRU), lambda t: (out(t), 0)),
            const((D_LRU, D_MODEL)),
            const((D_SB, D_MODEL)),
            pl.BlockSpec((1, D_MODEL), lambda t: (0, 0)),
        ],
        out_specs=pl.BlockSpec((tg, D_MODEL), lambda t: (out(t), 0)),
        out_shape=jax.ShapeDtypeStruct((S, D_MODEL), F32),
        scratch_shapes=[
            pltpu.VMEM((2, tg, D_SB), BF16),
            pltpu.VMEM((SB_HEADS * nsub, tq, tq), F32),
            pltpu.VMEM((SB_HEADS * nsub, tq, SB_HEAD_DIM), F32),
            pltpu.VMEM((tq, SB_HEAD_DIM), BF16),
            pltpu.VMEM((tq, SB_HEAD_DIM), BF16),
            pltpu.VMEM((tq, SB_HEAD_DIM), BF16),
            pltpu.SemaphoreType.DMA((3,)),
        ],
        compiler_params=pltpu.CompilerParams(
            dimension_semantics=("arbitrary",), vmem_limit_bytes=VMEM_LIMIT),
        name="sb_attn_outproj",
    )(proj, proj, proj, proj, proj, proj, proj, b_merge, proj, x, y_a, w_a, w_b, final_g)


def kernel(x, norm_g, w_in, conv_w, conv_b, w_rg, b_rg, w_ig, b_ig, lru_lambda, b_merge, w_out, final_g):
    B, S, _ = x.shape
    depth = norm_g.shape[0]
    outs = []
    for b in range(B):
        xb = x[b]
        for l in range(depth):
            w_in_l = w_in[l].astype(BF16)
            w_gates = (0.5 * jnp.concatenate([w_rg[l], w_ig[l]], axis=-1)).astype(BF16)
            w_a = w_out[l, :D_LRU].astype(BF16)
            w_b = w_out[l, D_LRU:].astype(BF16)
            row = lambda v: v.reshape(1, -1)

            proj = _inproj(xb, row(norm_g[l]), w_in_l)
            y_a = _lru(proj, conv_w[l], row(conv_b[l]), w_gates, row(b_rg[l]), row(b_ig[l]),
                       row(lru_lambda[l]), row(b_merge[l]))
            xb = _attn_outproj(proj, row(b_merge[l]), xb, y_a, w_a, w_b, row(final_g),
                               final=(l == depth - 1))
        outs.append(xb)
    return outs[0][None] if B == 1 else jnp.stack(outs, axis=0)
```

```python
import functools

import jax
import jax.numpy as jnp
from jax import lax
from jax.experimental import pallas as pl
from jax.experimental.pallas import tpu as pltpu

D_MODEL = 2048
D_LRU = 2048
LRU_BLOCKS = 16
LRU_BLOCK_W = 128
CONV_W = 4
LRU_C = 8.0
SB_HEADS = 16
SB_HEAD_DIM = 128
D_SB = SB_HEADS * SB_HEAD_DIM
D_CAT = D_LRU + D_SB
N_IN = 2 * D_LRU + 4 * D_SB + D_CAT
EPS = 1e-6

OFF_LRU_X = 0
OFF_LRU_GATE = OFF_LRU_X + D_LRU
OFF_Q = OFF_LRU_GATE + D_LRU
OFF_K = OFF_Q + D_SB
OFF_V = OFF_K + D_SB
OFF_SB_GATE = OFF_V + D_SB
OFF_MERGE_A = OFF_SB_GATE + D_SB
OFF_MERGE_B = OFF_MERGE_A + D_LRU

LANE = 128
SUBLANE = 8
VMEM_LIMIT = 56 * 1024 * 1024

EXIT_LOG2 = -150.5
LOG2_E = 1.4426950408889634

F32 = jnp.float32
BF16 = jnp.bfloat16


def _softplus2(z2):
    return jnp.maximum(z2, 0.0) + jnp.log2(1.0 + jnp.exp2(-jnp.abs(z2)))


def _sigmoid(z):
    return 0.5 * jnp.tanh(0.5 * z) + 0.5


def _silu(z):
    return z * _sigmoid(z)


def _inproj_kernel(x_ref, g_ref, w_ref, o_ref, h_ref):
    @pl.when(pl.program_id(1) == 0)
    def _():
        x = x_ref[...]
        ms = jnp.mean(x * x, axis=-1, keepdims=True)
        h_ref[...] = (x * lax.rsqrt(ms + EPS) * g_ref[...]).astype(BF16)

    o_ref[...] = jnp.dot(h_ref[...], w_ref[...].astype(BF16), preferred_element_type=F32).astype(BF16)


def _inproj(x, norm_g, w_in, tm=1024, tn=1024):
    S = x.shape[0]
    return pl.pallas_call(
        _inproj_kernel,
        grid=(S // tm, N_IN // tn),
        in_specs=[
            pl.BlockSpec((tm, D_MODEL), lambda i, j: (i, 0)),
            pl.BlockSpec((1, D_MODEL), lambda i, j: (0, 0)),
            pl.BlockSpec((D_MODEL, tn), lambda i, j: (0, j)),
        ],
        out_specs=pl.BlockSpec((tm, tn), lambda i, j: (i, j)),
        out_shape=jax.ShapeDtypeStruct((S, N_IN), BF16),
        scratch_shapes=[pltpu.VMEM((tm, D_MODEL), BF16)],
        compiler_params=pltpu.CompilerParams(
            dimension_semantics=("arbitrary", "arbitrary"), vmem_limit_bytes=VMEM_LIMIT),
        name="inproj",
    )(x, norm_g, w_in)


def _lru_kernel(px_ref, pg_ref, pm_ref, cw_ref, cb_ref, wg_ref, brg_ref, big_ref, lam_ref, bm_ref,
                y_ref, xs_ref, hs_ref, hc_ref, *, tl):
    i = pl.program_id(0)
    ng = tl // SUBLANE

    @pl.when(i == 0)
    def _():
        xs_ref[:, 0:SUBLANE, :] = jnp.zeros((LRU_BLOCKS, SUBLANE, LRU_BLOCK_W), F32)
        hc_ref[...] = jnp.zeros((SUBLANE, D_LRU), F32)

    lam = lam_ref[...]
    log_sig_lam = -(jnp.maximum(-lam, 0.0) + jnp.log1p(jnp.exp(-jnp.abs(lam))))
    half_c_lsl = (0.5 * LRU_C) * log_sig_lam
    sub = lax.broadcasted_iota(jnp.int32, (SUBLANE, LRU_BLOCK_W), 0)

    for c in range(LRU_BLOCKS):
        cs = slice(c * LRU_BLOCK_W, (c + 1) * LRU_BLOCK_W)
        xs_ref[c, SUBLANE:SUBLANE + tl, :] = px_ref[:, cs].astype(F32)
        pieces = []
        for g in range(ng):
            acc = cb_ref[:, cs]
            for tap in range(CONV_W):
                start = SUBLANE + g - (CONV_W - 1) + tap
                acc = acc + xs_ref[c, pl.ds(start, SUBLANE, stride=ng), :] * cw_ref[tap:tap + 1, cs]
            pieces.append(acc)
        xc = jnp.concatenate(pieces, axis=0)

        gates = jnp.dot(xc.astype(BF16), wg_ref[c], preferred_element_type=F32)
        r_t = jnp.tanh(gates[:, :LRU_BLOCK_W] + 0.5 * brg_ref[:, cs])
        i_t = jnp.tanh(gates[:, LRU_BLOCK_W:] + 0.5 * big_ref[:, cs])
        log_a = half_c_lsl[:, cs] * r_t + half_c_lsl[:, cs]
        a = jnp.exp(log_a)
        t = jnp.tanh(log_a)
        om = -0.5 * t / (1.0 - t)
        root = jnp.where(om > 0.0, om * lax.rsqrt(om), 0.0)
        u = root * ((i_t + 1.0) * xc)

        rg = lambda v, g: v[g * SUBLANE:(g + 1) * SUBLANE]
        h = rg(u, 0)
        p = rg(a, 0)
        hs, ps = [h], [p]
        for g in range(1, ng):
            h = rg(a, g) * h + rg(u, g)
            p = rg(a, g) * p
            hs.append(h)
            ps.append(p)

        pa, hb = p, h
        for d in (1, 2, 4):
            keep = sub >= d
            hb = hb + pa * jnp.where(keep, pltpu.roll(hb, d, axis=0), 0.0)
            pa = pa * jnp.where(keep, pltpu.roll(pa, d, axis=0), 1.0)
        h_in = hc_ref[:, cs]
        h_end = pa * h_in + hb
        seg_in = jnp.where(sub >= 1, pltpu.roll(h_end, 1, axis=0), h_in)
        hc_ref[:, cs] = jnp.broadcast_to(h_end[SUBLANE - 1:SUBLANE], (SUBLANE, LRU_BLOCK_W))

        for g in range(ng):
            hs_ref[c, pl.ds(g, SUBLANE, stride=ng), :] = hs[g] + ps[g] * seg_in
        xs_ref[c, 0:SUBLANE, :] = xs_ref[c, tl:tl + SUBLANE, :]

        gate = (_silu(pg_ref[:, cs].astype(F32))
                * _sigmoid(pm_ref[:, cs].astype(F32) + bm_ref[:, cs]))
        y_ref[:, cs] = (hs_ref[c] * gate).astype(BF16)


def _lru(proj, conv_w, conv_b, w_gates, b_rg, b_ig, lam, b_merge, tl=256):
    S = proj.shape[0]
    row_spec = lambda off: pl.BlockSpec((tl, D_LRU), lambda i, o=off // D_LRU: (i, o))
    vec_spec = pl.BlockSpec((1, D_LRU), lambda i: (0, 0))
    return pl.pallas_call(
        functools.partial(_lru_kernel, tl=tl),
        grid=(S // tl,),
        in_specs=[
            row_spec(OFF_LRU_X), row_spec(OFF_LRU_GATE), row_spec(OFF_MERGE_A),
            pl.BlockSpec((CONV_W, D_LRU), lambda i: (0, 0)),
            vec_spec,
            pl.BlockSpec((LRU_BLOCKS, LRU_BLOCK_W, 2 * LRU_BLOCK_W), lambda i: (0, 0, 0)),
            vec_spec, vec_spec, vec_spec,
            pl.BlockSpec((1, D_LRU), lambda i: (0, 0)),
        ],
        out_specs=pl.BlockSpec((tl, D_LRU), lambda i: (i, 0)),
        out_shape=jax.ShapeDtypeStruct((S, D_LRU), BF16),
        scratch_shapes=[
            pltpu.VMEM((LRU_BLOCKS, tl + SUBLANE, LRU_BLOCK_W), F32),
            pltpu.VMEM((LRU_BLOCKS, tl, LRU_BLOCK_W), F32),
            pltpu.VMEM((SUBLANE, D_LRU), F32),
        ],
        compiler_params=pltpu.CompilerParams(
            dimension_semantics=("arbitrary",), vmem_limit_bytes=VMEM_LIMIT),
        name="rglru",
    )(proj, proj, proj, conv_w, conv_b, w_gates, b_rg, b_ig, lam, b_merge)


_QK_DIMS = (((1,), (1,)), ((), ()))


def _attn_out_kernel(q_ref, kp_ref, kc_ref, vp_ref, vc_ref, sg_ref, mg_ref, bm_ref, proj_hbm,
                     x_ref, ya_ref, wa_ref, wb_ref, fg_ref, o_ref,
                     yb_ref, rem_ref, acc_ref, q_buf, k_buf, v_buf, sem, *, nsub, tq, hpg, ngroups, final):
    t = pl.program_id(0)
    g = jnp.minimum(t, ngroups - 1)
    slot = lax.rem(t, 2)
    near = nsub + 1
    win = near * tq
    scale = SB_HEAD_DIM ** -0.5 * LOG2_E
    ri = lax.broadcasted_iota(jnp.int32, (tq, tq), 0)
    ci = lax.broadcasted_iota(jnp.int32, (tq, tq), 1)
    causal = ci < ri
    neg_tri = -jnp.concatenate(
        [jnp.where(ri > ci, 1.0, 0.0), jnp.ones((tq, tq), F32)], axis=1).astype(BF16)
    neg_tri = jnp.concatenate([neg_tri, neg_tri], axis=0)
    prev_bias = jnp.where(g > 0, 0.0, -1e30)

    def split(sp):
        hi = sp.astype(BF16)
        lo = (sp - hi.astype(F32)).astype(BF16)
        return hi, lo

    def gate(rows, cols):
        return (_silu(sg_ref[rows, cols].astype(F32))
                * _sigmoid(mg_ref[rows, cols].astype(F32) + bm_ref[:, cols]))

    def near_field(heads):
        units = [(h, r) for h in heads for r in range(nsub)]
        zs, halves = [], []
        for h, r in units:
            cols = slice(h * SB_HEAD_DIM, (h + 1) * SB_HEAD_DIM)
            q = q_ref[r * tq:(r + 1) * tq, cols]
            zp = lax.dot_general(q, kp_ref[r * tq:, cols], _QK_DIMS, preferred_element_type=F32)
            zc = lax.dot_general(q, kc_ref[:(r + 1) * tq, cols], _QK_DIMS, preferred_element_type=F32)
            z = jnp.concatenate([zp * scale + prev_bias, zc * scale], axis=1)
            sp = _softplus2(z)
            sp = jnp.concatenate([sp[:, :win - tq], jnp.where(causal, sp[:, win - tq:], 0.0)], axis=1)
            zs.append(z - sp)
            hi, lo = split(sp)
            for b in range(near):
                bc = slice(b * tq, (b + 1) * tq)
                halves.append(jnp.concatenate([hi[:, bc], lo[:, bc]], axis=1))
        sums = jnp.dot(jnp.concatenate(halves, axis=0), neg_tri, preferred_element_type=F32)
        rem_max = None
        for u, (h, r) in enumerate(units):
            cols = slice(h * SB_HEAD_DIM, (h + 1) * SB_HEAD_DIM)
            rows = slice(r * tq, (r + 1) * tq)
            rem = None
            ws = [None] * near
            for b in reversed(range(near)):
                s = sums[(u * near + b) * tq:(u * near + b + 1) * tq]
                e = zs[u][:, b * tq:(b + 1) * tq] + s[:, :tq]
                w = jnp.exp2(e if rem is None else e + rem)
                if b == near - 1:
                    w = jnp.where(causal, w, 0.0)
                ws[b] = w.astype(BF16)
                rem = s[:, tq:] if rem is None else rem + s[:, tq:]
            w = jnp.concatenate(ws, axis=1)
            n_prev = (nsub - r) * tq
            acc = (jnp.dot(w[:, :n_prev], vp_ref[r * tq:, cols], preferred_element_type=F32)
                   + jnp.dot(w[:, n_prev:], vc_ref[:(r + 1) * tq, cols], preferred_element_type=F32))
            rem_ref[h * nsub + r] = rem
            acc_ref[h * nsub + r] = acc
            yb_ref[slot, rows, cols] = (acc * gate(rows, cols)).astype(BF16)
            rem_max = rem if rem_max is None else jnp.maximum(rem_max, rem)
        return rem_max

    def fetch(dst, slot, row0, col0):
        cp = pltpu.make_async_copy(
            proj_hbm.at[pl.ds(row0, tq), pl.ds(col0, SB_HEAD_DIM)], dst, sem.at[slot])
        cp.start()
        return cp

    def far_field():
        def unit(idx, carry):
            h = idx // nsub
            r = idx - h * nsub
            kb0 = g * nsub + r - near
            rem_max0 = jnp.max(rem_ref[idx])

            @pl.when(jnp.logical_and(kb0 >= 0, rem_max0 >= EXIT_LOG2))
            def _():
                hcol = h * SB_HEAD_DIM
                fetch(q_buf, 0, pl.multiple_of((g * nsub + r) * tq, tq),
                      pl.multiple_of(OFF_Q + hcol, SB_HEAD_DIM)).wait()

                def cond(c):
                    kb, rem_max = c
                    return jnp.logical_and(kb >= 0, rem_max >= EXIT_LOG2)

                def body(c):
                    kb, _ = c
                    row0 = pl.multiple_of(kb * tq, tq)
                    ck = fetch(k_buf, 1, row0, pl.multiple_of(OFF_K + hcol, SB_HEAD_DIM))
                    cv = fetch(v_buf, 2, row0, pl.multiple_of(OFF_V + hcol, SB_HEAD_DIM))
                    ck.wait()
                    cv.wait()
                    z = lax.dot_general(q_buf[...], k_buf[...], _QK_DIMS, preferred_element_type=F32) * scale
                    sp = _softplus2(z)
                    hi, lo = split(sp)
                    sums = jnp.dot(jnp.concatenate([hi, lo], axis=1), neg_tri, preferred_element_type=F32)
                    rem = rem_ref[idx]
                    w = jnp.exp2((z - sp) + (rem + sums[:, :tq]))
                    acc_ref[idx] = acc_ref[idx] + jnp.dot(w.astype(BF16), v_buf[...], preferred_element_type=F32)
                    rem = rem + sums[:, tq:]
                    rem_ref[idx] = rem
                    return kb - 1, jnp.max(rem)

                lax.while_loop(cond, body, (kb0, rem_max0))

            return carry

        lax.fori_loop(0, SB_HEADS * nsub, unit, 0)
        for h in range(SB_HEADS):
            cols = slice(h * SB_HEAD_DIM, (h + 1) * SB_HEAD_DIM)
            for r in range(nsub):
                rows = slice(r * tq, (r + 1) * tq)
                yb_ref[slot, rows, cols] = (acc_ref[h * nsub + r] * gate(rows, cols)).astype(BF16)

    nchunks = SB_HEADS // hpg
    cn = D_MODEL // nchunks

    def outproj_chunk(i):
        cc = slice(i * cn, (i + 1) * cn)
        acc = jnp.dot(ya_ref[...], wa_ref[:, cc], preferred_element_type=F32)
        acc = acc + jnp.dot(yb_ref[1 - slot], wb_ref[:, cc], preferred_element_type=F32)
        xr = x_ref[:, cc] + acc
        o_ref[:, cc] = xr
        return jnp.sum(xr * xr, axis=1, keepdims=True)

    @pl.when(t == 0)
    def _():
        yb_ref[1] = jnp.zeros(yb_ref.shape[1:], BF16)

    rem_max = None
    ssq = None
    for i, h0 in enumerate(range(0, SB_HEADS, hpg)):
        m = near_field(range(h0, h0 + hpg))
        rem_max = m if rem_max is None else jnp.maximum(rem_max, m)
        s = outproj_chunk(i)
        ssq = s if ssq is None else ssq + s
    if final:
        o_ref[...] = o_ref[...] * lax.rsqrt(ssq * (1.0 / D_MODEL) + EPS) * fg_ref[...]

    @pl.when(jnp.max(rem_max) >= EXIT_LOG2)
    def _():
        far_field()


def _attn_outproj(proj, b_merge, x, y_a, w_out, final_g, final, nsub=2, tq=128, hpg=2):
    S = proj.shape[0]
    assert D_LRU == D_SB, "w_out is split into two equal row blocks, one per branch"
    tg = nsub * tq
    ngroups = S // tg
    cb = lambda off: off // D_SB
    att = lambda t: jnp.minimum(t, ngroups - 1)
    out = lambda t: jnp.maximum(t - 1, 0)
    cur = lambda off: pl.BlockSpec((tg, D_SB), lambda t, o=cb(off): (att(t), o))
    prev = lambda off: pl.BlockSpec((tg, D_SB), lambda t, o=cb(off): (jnp.maximum(att(t) - 1, 0), o))
    const = lambda shape, rb: pl.BlockSpec(shape, lambda t: (rb, 0), pipeline_mode=pl.Buffered(1))
    return pl.pallas_call(
        functools.partial(_attn_out_kernel, nsub=nsub, tq=tq, hpg=hpg, ngroups=ngroups, final=final),
        grid=(ngroups + 1,),
        in_specs=[
            cur(OFF_Q), prev(OFF_K), cur(OFF_K), prev(OFF_V), cur(OFF_V),
            cur(OFF_SB_GATE), cur(OFF_MERGE_B),
            pl.BlockSpec((1, D_SB), lambda t: (0, cb(D_LRU))),
            pl.BlockSpec(memory_space=pl.ANY),
            pl.BlockSpec((tg, D_MODEL), lambda t: (out(t), 0)),
            pl.BlockSpec((tg, D_LRU), lambda t: (out(t), 0)),
            const((D_LRU, D_MODEL), 0),
            const((D_SB, D_MODEL), 1),
            pl.BlockSpec((1, D_MODEL), lambda t: (0, 0)),
        ],
        out_specs=pl.BlockSpec((tg, D_MODEL), lambda t: (out(t), 0)),
        out_shape=jax.ShapeDtypeStruct((S, D_MODEL), F32),
        scratch_shapes=[
            pltpu.VMEM((2, tg, D_SB), BF16),
            pltpu.VMEM((SB_HEADS * nsub, tq, tq), F32),
            pltpu.VMEM((SB_HEADS * nsub, tq, SB_HEAD_DIM), F32),
            pltpu.VMEM((tq, SB_HEAD_DIM), BF16),
            pltpu.VMEM((tq, SB_HEAD_DIM), BF16),
            pltpu.VMEM((tq, SB_HEAD_DIM), BF16),
            pltpu.SemaphoreType.DMA((3,)),
        ],
        compiler_params=pltpu.CompilerParams(
            dimension_semantics=("arbitrary",), vmem_limit_bytes=VMEM_LIMIT),
        name="sb_attn_outproj",
    )(proj, proj, proj, proj, proj, proj, proj, b_merge, proj, x, y_a, w_out, w_out, final_g)


def kernel(x, norm_g, w_in, conv_w, conv_b, w_rg, b_rg, w_ig, b_ig, lru_lambda, b_merge, w_out, final_g):
    B, S, _ = x.shape
    depth = norm_g.shape[0]
    outs = []
    for b in range(B):
        xb = x[b]
        for l in range(depth):
            w_gates = (0.5 * jnp.concatenate([w_rg[l], w_ig[l]], axis=-1)).astype(BF16)
            w_out_l = w_out[l].astype(BF16)
            row = lambda v: v.reshape(1, -1)

            proj = _inproj(xb, row(norm_g[l]), w_in[l])
            y_a = _lru(proj, conv_w[l], row(conv_b[l]), w_gates, row(b_rg[l]), row(b_ig[l]),
                       row(lru_lambda[l]), row(b_merge[l]))
            xb = _attn_outproj(proj, row(b_merge[l]), xb, y_a, w_out_l, row(final_g),
                               final=(l == depth - 1))
        outs.append(xb)
    return outs[0][None] if B == 1 else jnp.stack(outs, axis=0)
```

```python
import functools

import jax
import jax.numpy as jnp
from jax import lax
from jax.experimental import pallas as pl
from jax.experimental.pallas import tpu as pltpu

D_MODEL = 2048
D_LRU = 2048
LRU_BLOCKS = 16
LRU_BLOCK_W = 128
CONV_W = 4
LRU_C = 8.0
SB_HEADS = 16
SB_HEAD_DIM = 128
D_SB = SB_HEADS * SB_HEAD_DIM
D_CAT = D_LRU + D_SB
N_IN = 2 * D_LRU + 4 * D_SB + D_CAT
EPS = 1e-6

OFF_LRU_X = 0
OFF_LRU_GATE = OFF_LRU_X + D_LRU
OFF_Q = OFF_LRU_GATE + D_LRU
OFF_K = OFF_Q + D_SB
OFF_V = OFF_K + D_SB
OFF_SB_GATE = OFF_V + D_SB
OFF_MERGE_A = OFF_SB_GATE + D_SB
OFF_MERGE_B = OFF_MERGE_A + D_LRU

LANE = 128
SUBLANE = 8
VMEM_LIMIT = 58 * 1024 * 1024

EXIT_LOG2 = -150.5
LOG2_E = 1.4426950408889634

F32 = jnp.float32
BF16 = jnp.bfloat16


def _softplus2(z2):
    return jnp.maximum(z2, 0.0) + jnp.log2(1.0 + jnp.exp2(-jnp.abs(z2)))


def _sigmoid(z):
    return 0.5 * jnp.tanh(0.5 * z) + 0.5


def _silu(z):
    return z * _sigmoid(z)


def _inproj_kernel(x_ref, g_ref, w_ref, o_ref, h_ref):
    @pl.when(pl.program_id(1) == 0)
    def _():
        x = x_ref[...]
        ms = jnp.mean(x * x, axis=-1, keepdims=True)
        h_ref[...] = (x * lax.rsqrt(ms + EPS) * g_ref[...]).astype(BF16)

    o_ref[...] = jnp.dot(h_ref[...], w_ref[...].astype(BF16), preferred_element_type=F32).astype(BF16)


def _inproj(x, norm_g, w_in, tm=2048, tn=512):
    S = x.shape[0]
    return pl.pallas_call(
        _inproj_kernel,
        grid=(S // tm, N_IN // tn),
        in_specs=[
            pl.BlockSpec((tm, D_MODEL), lambda i, j: (i, 0)),
            pl.BlockSpec((1, D_MODEL), lambda i, j: (0, 0)),
            pl.BlockSpec((D_MODEL, tn), lambda i, j: (0, j)),
        ],
        out_specs=pl.BlockSpec((tm, tn), lambda i, j: (i, j)),
        out_shape=jax.ShapeDtypeStruct((S, N_IN), BF16),
        scratch_shapes=[pltpu.VMEM((tm, D_MODEL), BF16)],
        compiler_params=pltpu.CompilerParams(
            dimension_semantics=("arbitrary", "arbitrary"), vmem_limit_bytes=VMEM_LIMIT),
        name="inproj",
    )(x, norm_g, w_in)


def _lru_kernel(px_ref, pg_ref, pm_ref, cw_ref, cb_ref, wg_ref, brg_ref, big_ref, lam_ref, bm_ref,
                y_ref, xs_ref, hs_ref, hc_ref, *, tl):
    i = pl.program_id(0)
    ng = tl // SUBLANE

    @pl.when(i == 0)
    def _():
        xs_ref[:, 0:SUBLANE, :] = jnp.zeros((LRU_BLOCKS, SUBLANE, LRU_BLOCK_W), F32)
        hc_ref[...] = jnp.zeros((SUBLANE, D_LRU), F32)

    lam = lam_ref[...]
    log_sig_lam = -(jnp.maximum(-lam, 0.0) + jnp.log1p(jnp.exp(-jnp.abs(lam))))
    half_c_lsl = (0.5 * LRU_C) * log_sig_lam
    sub = lax.broadcasted_iota(jnp.int32, (SUBLANE, LRU_BLOCK_W), 0)

    for c in range(LRU_BLOCKS):
        cs = slice(c * LRU_BLOCK_W, (c + 1) * LRU_BLOCK_W)
        xs_ref[c, SUBLANE:SUBLANE + tl, :] = px_ref[:, cs].astype(F32)
        pieces = []
        for g in range(ng):
            acc = cb_ref[:, cs]
            for tap in range(CONV_W):
                start = SUBLANE + g - (CONV_W - 1) + tap
                acc = acc + xs_ref[c, pl.ds(start, SUBLANE, stride=ng), :] * cw_ref[tap:tap + 1, cs]
            pieces.append(acc)
        xc = jnp.concatenate(pieces, axis=0)

        gates = jnp.dot(xc.astype(BF16), wg_ref[c], preferred_element_type=F32)
        r_t = jnp.tanh(gates[:, :LRU_BLOCK_W] + 0.5 * brg_ref[:, cs])
        i_t = jnp.tanh(gates[:, LRU_BLOCK_W:] + 0.5 * big_ref[:, cs])
        log_a = half_c_lsl[:, cs] * r_t + half_c_lsl[:, cs]
        a = jnp.exp(log_a)
        t = jnp.tanh(log_a)
        om = -0.5 * t / (1.0 - t)
        root = jnp.where(om > 0.0, om * lax.rsqrt(om), 0.0)
        u = root * ((i_t + 1.0) * xc)

        rg = lambda v, g: v[g * SUBLANE:(g + 1) * SUBLANE]
        h = rg(u, 0)
        p = rg(a, 0)
        hs, ps = [h], [p]
        for g in range(1, ng):
            h = rg(a, g) * h + rg(u, g)
            p = rg(a, g) * p
            hs.append(h)
            ps.append(p)

        pa, hb = p, h
        for d in (1, 2, 4):
            keep = sub >= d
            hb = hb + pa * jnp.where(keep, pltpu.roll(hb, d, axis=0), 0.0)
            pa = pa * jnp.where(keep, pltpu.roll(pa, d, axis=0), 1.0)
        h_in = hc_ref[:, cs]
        h_end = pa * h_in + hb
        seg_in = jnp.where(sub >= 1, pltpu.roll(h_end, 1, axis=0), h_in)
        hc_ref[:, cs] = jnp.broadcast_to(h_end[SUBLANE - 1:SUBLANE], (SUBLANE, LRU_BLOCK_W))

        for g in range(ng):
            hs_ref[c, pl.ds(g, SUBLANE, stride=ng), :] = hs[g] + ps[g] * seg_in
        xs_ref[c, 0:SUBLANE, :] = xs_ref[c, tl:tl + SUBLANE, :]

        gate = (_silu(pg_ref[:, cs].astype(F32))
                * _sigmoid(pm_ref[:, cs].astype(F32) + bm_ref[:, cs]))
        y_ref[:, cs] = (hs_ref[c] * gate).astype(BF16)


def _lru(proj, conv_w, conv_b, w_gates, b_rg, b_ig, lam, b_merge, tl=256):
    S = proj.shape[0]
    row_spec = lambda off: pl.BlockSpec((tl, D_LRU), lambda i, o=off // D_LRU: (i, o))
    vec_spec = pl.BlockSpec((1, D_LRU), lambda i: (0, 0))
    return pl.pallas_call(
        functools.partial(_lru_kernel, tl=tl),
        grid=(S // tl,),
        in_specs=[
            row_spec(OFF_LRU_X), row_spec(OFF_LRU_GATE), row_spec(OFF_MERGE_A),
            pl.BlockSpec((CONV_W, D_LRU), lambda i: (0, 0)),
            vec_spec,
            pl.BlockSpec((LRU_BLOCKS, LRU_BLOCK_W, 2 * LRU_BLOCK_W), lambda i: (0, 0, 0)),
            vec_spec, vec_spec, vec_spec,
            pl.BlockSpec((1, D_LRU), lambda i: (0, 0)),
        ],
        out_specs=pl.BlockSpec((tl, D_LRU), lambda i: (i, 0)),
        out_shape=jax.ShapeDtypeStruct((S, D_LRU), BF16),
        scratch_shapes=[
            pltpu.VMEM((LRU_BLOCKS, tl + SUBLANE, LRU_BLOCK_W), F32),
            pltpu.VMEM((LRU_BLOCKS, tl, LRU_BLOCK_W), F32),
            pltpu.VMEM((SUBLANE, D_LRU), F32),
        ],
        compiler_params=pltpu.CompilerParams(
            dimension_semantics=("arbitrary",), vmem_limit_bytes=VMEM_LIMIT),
        name="rglru",
    )(proj, proj, proj, conv_w, conv_b, w_gates, b_rg, b_ig, lam, b_merge)


_QK_DIMS = (((1,), (1,)), ((), ()))


def _attn_out_kernel(q_ref, kp_ref, kc_ref, vp_ref, vc_ref, sg_ref, mg_ref, bm_ref, proj_hbm,
                     x_ref, ya_ref, wa_ref, wb_ref, fg_ref, o_ref,
                     yb_ref, rem_ref, acc_ref, q_buf, k_buf, v_buf, sem, *, nsub, tq, hpg, ngroups, final):
    t = pl.program_id(0)
    g = jnp.minimum(t, ngroups - 1)
    slot = lax.rem(t, 2)
    near = nsub + 1
    win = near * tq
    scale = SB_HEAD_DIM ** -0.5 * LOG2_E
    ri = lax.broadcasted_iota(jnp.int32, (tq, tq), 0)
    ci = lax.broadcasted_iota(jnp.int32, (tq, tq), 1)
    causal = ci < ri
    neg_tri = -jnp.concatenate(
        [jnp.where(ri > ci, 1.0, 0.0), jnp.ones((tq, tq), F32)], axis=1).astype(BF16)
    neg_tri = jnp.concatenate([neg_tri, neg_tri], axis=0)
    prev_bias = jnp.where(g > 0, 0.0, -1e30)

    def split(sp):
        hi = sp.astype(BF16)
        lo = (sp - hi.astype(F32)).astype(BF16)
        return hi, lo

    def gate(rows, cols):
        return (_silu(sg_ref[rows, cols].astype(F32))
                * _sigmoid(mg_ref[rows, cols].astype(F32) + bm_ref[:, cols]))

    def near_field(heads):
        units = [(h, r) for h in heads for r in range(nsub)]
        zs, halves = [], []
        for h, r in units:
            cols = slice(h * SB_HEAD_DIM, (h + 1) * SB_HEAD_DIM)
            q = q_ref[r * tq:(r + 1) * tq, cols]
            zp = lax.dot_general(q, kp_ref[r * tq:, cols], _QK_DIMS, preferred_element_type=F32)
            zc = lax.dot_general(q, kc_ref[:(r + 1) * tq, cols], _QK_DIMS, preferred_element_type=F32)
            z = jnp.concatenate([zp * scale + prev_bias, zc * scale], axis=1)
            sp = _softplus2(z)
            sp = jnp.concatenate([sp[:, :win - tq], jnp.where(causal, sp[:, win - tq:], 0.0)], axis=1)
            zs.append(z - sp)
            hi, lo = split(sp)
            for b in range(near):
                bc = slice(b * tq, (b + 1) * tq)
                halves.append(jnp.concatenate([hi[:, bc], lo[:, bc]], axis=1))
        sums = jnp.dot(jnp.concatenate(halves, axis=0), neg_tri, preferred_element_type=F32)
        rem_max = None
        for u, (h, r) in enumerate(units):
            cols = slice(h * SB_HEAD_DIM, (h + 1) * SB_HEAD_DIM)
            rows = slice(r * tq, (r + 1) * tq)
            rem = None
            ws = [None] * near
            for b in reversed(range(near)):
                s = sums[(u * near + b) * tq:(u * near + b + 1) * tq]
                e = zs[u][:, b * tq:(b + 1) * tq] + s[:, :tq]
                w = jnp.exp2(e if rem is None else e + rem)
                if b == near - 1:
                    w = jnp.where(causal, w, 0.0)
                ws[b] = w.astype(BF16)
                rem = s[:, tq:] if rem is None else rem + s[:, tq:]
            w = jnp.concatenate(ws, axis=1)
            n_prev = (nsub - r) * tq
            acc = (jnp.dot(w[:, :n_prev], vp_ref[r * tq:, cols], preferred_element_type=F32)
                   + jnp.dot(w[:, n_prev:], vc_ref[:(r + 1) * tq, cols], preferred_element_type=F32))
            rem_ref[h * nsub + r] = rem
            acc_ref[h * nsub + r] = acc
            yb_ref[slot, rows, cols] = (acc * gate(rows, cols)).astype(BF16)
            rem_max = rem if rem_max is None else jnp.maximum(rem_max, rem)
        return rem_max

    def fetch(dst, slot, row0, col0):
        cp = pltpu.make_async_copy(
            proj_hbm.at[pl.ds(row0, tq), pl.ds(col0, SB_HEAD_DIM)], dst, sem.at[slot])
        cp.start()
        return cp

    def far_field():
        def unit(idx, carry):
            h = idx // nsub
            r = idx - h * nsub
            kb0 = g * nsub + r - near
            rem_max0 = jnp.max(rem_ref[idx])

            @pl.when(jnp.logical_and(kb0 >= 0, rem_max0 >= EXIT_LOG2))
            def _():
                hcol = h * SB_HEAD_DIM
                fetch(q_buf, 0, pl.multiple_of((g * nsub + r) * tq, tq),
                      pl.multiple_of(OFF_Q + hcol, SB_HEAD_DIM)).wait()

                def cond(c):
                    kb, rem_max = c
                    return jnp.logical_and(kb >= 0, rem_max >= EXIT_LOG2)

                def body(c):
                    kb, _ = c
                    row0 = pl.multiple_of(kb * tq, tq)
                    ck = fetch(k_buf, 1, row0, pl.multiple_of(OFF_K + hcol, SB_HEAD_DIM))
                    cv = fetch(v_buf, 2, row0, pl.multiple_of(OFF_V + hcol, SB_HEAD_DIM))
                    ck.wait()
                    cv.wait()
                    z = lax.dot_general(q_buf[...], k_buf[...], _QK_DIMS, preferred_element_type=F32) * scale
                    sp = _softplus2(z)
                    hi, lo = split(sp)
                    sums = jnp.dot(jnp.concatenate([hi, lo], axis=1), neg_tri, preferred_element_type=F32)
                    rem = rem_ref[idx]
                    w = jnp.exp2((z - sp) + (rem + sums[:, :tq]))
                    acc_ref[idx] = acc_ref[idx] + jnp.dot(w.astype(BF16), v_buf[...], preferred_element_type=F32)
                    rem = rem + sums[:, tq:]
                    rem_ref[idx] = rem
                    return kb - 1, jnp.max(rem)

                lax.while_loop(cond, body, (kb0, rem_max0))

            return carry

        lax.fori_loop(0, SB_HEADS * nsub, unit, 0)
        for h in range(SB_HEADS):
            cols = slice(h * SB_HEAD_DIM, (h + 1) * SB_HEAD_DIM)
            for r in range(nsub):
                rows = slice(r * tq, (r + 1) * tq)
                yb_ref[slot, rows, cols] = (acc_ref[h * nsub + r] * gate(rows, cols)).astype(BF16)

    nchunks = SB_HEADS // hpg
    cn = D_MODEL // nchunks

    def outproj_chunk(i):
        cc = slice(i * cn, (i + 1) * cn)
        acc = jnp.dot(ya_ref[...], wa_ref[:, cc], preferred_element_type=F32)
        acc = acc + jnp.dot(yb_ref[1 - slot], wb_ref[:, cc], preferred_element_type=F32)
        xr = x_ref[:, cc] + acc
        o_ref[:, cc] = xr
        return jnp.sum(xr * xr, axis=1, keepdims=True)

    @pl.when(t == 0)
    def _():
        yb_ref[1] = jnp.zeros(yb_ref.shape[1:], BF16)

    rem_max = None
    ssq = None
    for i, h0 in enumerate(range(0, SB_HEADS, hpg)):
        m = near_field(range(h0, h0 + hpg))
        rem_max = m if rem_max is None else jnp.maximum(rem_max, m)
        s = outproj_chunk(i)
        ssq = s if ssq is None else ssq + s
    if final:
        o_ref[...] = o_ref[...] * lax.rsqrt(ssq * (1.0 / D_MODEL) + EPS) * fg_ref[...]

    @pl.when(jnp.max(rem_max) >= EXIT_LOG2)
    def _():
        far_field()


def _attn_outproj(proj, b_merge, x, y_a, w_out, final_g, final, nsub=2, tq=128, hpg=2):
    S = proj.shape[0]
    assert D_LRU == D_SB, "w_out is split into two equal row blocks, one per branch"
    tg = nsub * tq
    ngroups = S // tg
    cb = lambda off: off // D_SB
    att = lambda t: jnp.minimum(t, ngroups - 1)
    out = lambda t: jnp.maximum(t - 1, 0)
    cur = lambda off: pl.BlockSpec((tg, D_SB), lambda t, o=cb(off): (att(t), o))
    prev = lambda off: pl.BlockSpec((tg, D_SB), lambda t, o=cb(off): (jnp.maximum(att(t) - 1, 0), o))
    const = lambda shape, rb: pl.BlockSpec(shape, lambda t: (rb, 0), pipeline_mode=pl.Buffered(1))
    return pl.pallas_call(
        functools.partial(_attn_out_kernel, nsub=nsub, tq=tq, hpg=hpg, ngroups=ngroups, final=final),
        grid=(ngroups + 1,),
        in_specs=[
            cur(OFF_Q), prev(OFF_K), cur(OFF_K), prev(OFF_V), cur(OFF_V),
            cur(OFF_SB_GATE), cur(OFF_MERGE_B),
            pl.BlockSpec((1, D_SB), lambda t: (0, cb(D_LRU))),
            pl.BlockSpec(memory_space=pl.ANY),
            pl.BlockSpec((tg, D_MODEL), lambda t: (out(t), 0)),
            pl.BlockSpec((tg, D_LRU), lambda t: (out(t), 0)),
            const((D_LRU, D_MODEL), 0),
            const((D_SB, D_MODEL), 1),
            pl.BlockSpec((1, D_MODEL), lambda t: (0, 0)),
        ],
        out_specs=pl.BlockSpec((tg, D_MODEL), lambda t: (out(t), 0)),
        out_shape=jax.ShapeDtypeStruct((S, D_MODEL), F32),
        scratch_shapes=[
            pltpu.VMEM((2, tg, D_SB), BF16),
            pltpu.VMEM((SB_HEADS * nsub, tq, tq), F32),
            pltpu.VMEM((SB_HEADS * nsub, tq, SB_HEAD_DIM), F32),
            pltpu.VMEM((tq, SB_HEAD_DIM), BF16),
            pltpu.VMEM((tq, SB_HEAD_DIM), BF16),
            pltpu.VMEM((tq, SB_HEAD_DIM), BF16),
            pltpu.SemaphoreType.DMA((3,)),
        ],
        compiler_params=pltpu.CompilerParams(
            dimension_semantics=("arbitrary",), vmem_limit_bytes=VMEM_LIMIT),
        name="sb_attn_outproj",
    )(proj, proj, proj, proj, proj, proj, proj, b_merge, proj, x, y_a, w_out, w_out, final_g)


def kernel(x, norm_g, w_in, conv_w, conv_b, w_rg, b_rg, w_ig, b_ig, lru_lambda, b_merge, w_out, final_g):
    B, S, _ = x.shape
    depth = norm_g.shape[0]
    outs = []
    for b in range(B):
        xb = x[b]
        for l in range(depth):
            w_gates = (0.5 * jnp.concatenate([w_rg[l], w_ig[l]], axis=-1)).astype(BF16)
            w_out_l = w_out[l].astype(BF16)
            row = lambda v: v.reshape(1, -1)

            proj = _inproj(xb, row(norm_g[l]), w_in[l])
            y_a = _lru(proj, conv_w[l], row(conv_b[l]), w_gates, row(b_rg[l]), row(b_ig[l]),
                       row(lru_lambda[l]), row(b_merge[l]))
            xb = _attn_outproj(proj, row(b_merge[l]), xb, y_a, w_out_l, row(final_g),
                               final=(l == depth - 1))
        outs.append(xb)
    return outs[0][None] if B == 1 else jnp.stack(outs, axis=0)
```
